```python
import math
import jax, jax.numpy as jnp
from jax import lax
import numpy as np

D_MODEL = 1024
BATCH = 4
SEQ = 4096
DEPTH = 2
DEC_BATCH = 32
DEC_SEQ = 4
PAST_LEN = 8192
PAGE_SIZE = 128

N_HEADS = 8
HEAD_DIM = 128
KV_HEADS = 4
Q_PER_KV = N_HEADS // KV_HEADS
ATT_DIM = N_HEADS * HEAD_DIM
KV_DIM = KV_HEADS * HEAD_DIM
ATT_SCALE = HEAD_DIM ** -0.5
IDX_HEADS = 8
IDX_DIM = 64
IDX_ROPE_DIM = 32
IDX_SCALE = IDX_HEADS ** -0.5 * IDX_DIM ** -0.5
INDEX_TOPK = 256
ATT_QBLOCK = 128
ROPE_THETA = 10000.0
D_INNER = 2 * D_MODEL
SSD_HEAD_DIM = 64
SSD_HEADS = D_INNER // SSD_HEAD_DIM
SSD_GROUPS = 4
HEADS_PER_GROUP = SSD_HEADS // SSD_GROUPS
D_STATE = 128
CONV_W = 4
CONV_CH = D_INNER + 2 * SSD_GROUPS * D_STATE
SSD_CHUNK = 128
FFN_DIM = 2816
N_MODS = 9
EPS = 1e-6
IN_SPLITS = (ATT_DIM, KV_DIM, KV_DIM, IDX_HEADS * IDX_DIM, IDX_DIM, IDX_HEADS,
             D_INNER, CONV_CH, SSD_HEADS, D_MODEL, D_MODEL)
IN_DIM = sum(IN_SPLITS)

kernel_name = 'hybrid_dsa_ssd_macaron_adaln_step'

F32 = jnp.float32


def _split(t, sizes):
    return jnp.split(t, np.cumsum(sizes)[:-1].tolist(), axis=-1)


def _rmsnorm(x, g):
    xf = x.astype(F32)
    xf = xf * lax.rsqrt(jnp.mean(xf * xf, axis=-1, keepdims=True) + EPS)
    return (xf * g.astype(F32)).astype(x.dtype)


def _modulate(xn, shift, scale):
    return xn * (1 + scale[:, None, :]) + shift[:, None, :]


def _swiglu(h, w_up, w_down):
    gate, up = jnp.split(h @ w_up, 2, axis=-1)
    return (jax.nn.silu(gate) * up) @ w_down


def _rope(t, pos):
    half = t.shape[-1] // 2
    inv_freq = ROPE_THETA ** (-jnp.arange(half, dtype=F32) / half)
    ang = pos.astype(F32)[:, None] * inv_freq[None, :]
    cos = jnp.cos(ang)[None, :, None, :]
    sin = jnp.sin(ang)[None, :, None, :]
    t1 = t[..., :half].astype(F32)
    t2 = t[..., half:].astype(F32)
    return jnp.concatenate([t1 * cos - t2 * sin, t2 * cos + t1 * sin], axis=-1).astype(t.dtype)


def _rope_partial(t, pos):
    return jnp.concatenate([_rope(t[..., :IDX_ROPE_DIM], pos), t[..., IDX_ROPE_DIM:]], axis=-1)


def _gather_pages(pool, page_table):
    g = pool[page_table]
    return g.reshape(page_table.shape[0], page_table.shape[1] * pool.shape[1], *pool.shape[2:])


def _dsa_attention(q, iq, iw, pos, k_all, v_all, ik_all, n_top):
    bsz, lq = q.shape[:2]
    qb = ATT_QBLOCK if lq % ATT_QBLOCK == 0 else lq
    nb = lq // qb
    key_pos = jnp.arange(k_all.shape[1], dtype=jnp.int32)
    ik_f = ik_all.astype(F32)

    def blocks(t):
        return jnp.moveaxis(t.reshape(bsz, nb, qb, *t.shape[2:]), 1, 0)

    def one_block(args):
        q_b, iq_b, iw_b, p_b = args
        logits = jnp.einsum('bqhd,bsd->bqhs', iq_b.astype(F32), ik_f)
        score = jnp.einsum('bqhs,bqh->bqs', jax.nn.relu(logits), iw_b.astype(F32))
        admissible = key_pos[None, :] <= p_b[:, None]
        score = jnp.where(admissible[None], score, -jnp.inf)
        top_val, top_idx = lax.top_k(score, n_top)
        kg = jax.vmap(lambda kk, ii: kk[ii])(k_all, top_idx)
        vg = jax.vmap(lambda vv, ii: vv[ii])(v_all, top_idx)
        qg = q_b.reshape(bsz, qb, KV_HEADS, Q_PER_KV, HEAD_DIM)
        s = jnp.einsum('bqkgd,bqnkd->bqkgn', qg, kg).astype(F32) * ATT_SCALE
        s = jnp.where(jnp.isfinite(top_val)[:, :, None, None, :], s, -jnp.inf)
        prob = jax.nn.softmax(s, axis=-1).astype(vg.dtype)
        o = jnp.einsum('bqkgn,bqnkd->bqkgd', prob, vg)
        return o.reshape(bsz, qb, ATT_DIM)

    out = lax.map(one_block, (blocks(q), blocks(iq), blocks(iw), pos.reshape(nb, qb)))
    return jnp.moveaxis(out, 0, 1).reshape(bsz, lq, ATT_DIM)


def _ssd_scan(x, dt, a_neg, bm, cm, h0, chunk):
    bsz, seq = x.shape[:2]
    nc = seq // chunk

    def r(t):
        return t.reshape(bsz, nc, chunk, *t.shape[2:])

    xc, dtc, bc, cc = r(x), r(dt), r(bm), r(cm)
    acum = jnp.cumsum(dtc * a_neg, axis=2)
    seg = acum[:, :, :, None, :] - acum[:, :, None, :, :]
    causal = jnp.tril(jnp.ones((chunk, chunk), dtype=bool))[None, None, :, :, None]
    decay = jnp.exp(jnp.where(causal, seg, -jnp.inf))
    cb = jnp.einsum('bclhn,bcshn->bclsh', cc, bc)
    y_diag = jnp.einsum('bclsh,bcsh,bcshp->bclhp', cb * decay, dtc, xc)
    decay_end = jnp.exp(acum[:, :, -1:, :] - acum)
    states = jnp.einsum('bcshn,bcsh,bcshp->bchpn', bc, decay_end * dtc, xc)
    chunk_decay = jnp.exp(acum[:, :, -1, :])

    def step(h, inp):
        st, dec = inp
        return h * dec[:, :, None, None] + st, h

    h_fin, h_prev = lax.scan(step, h0, (jnp.moveaxis(states, 1, 0), jnp.moveaxis(chunk_decay, 1, 0)))
    h_prev = jnp.moveaxis(h_prev, 0, 1)
    y_off = jnp.einsum('bclhn,bchpn,bclh->bclhp', cc, h_prev, jnp.exp(acum))
    return (y_diag + y_off).reshape(bsz, seq, *x.shape[2:]), h_fin


def _ssd_branch(z, xbc, dt_raw, conv_state, ssm_state, p, l):
    bsz, lq = xbc.shape[:2]
    if conv_state is None:
        conv_state = jnp.zeros((bsz, CONV_W - 1, CONV_CH), xbc.dtype)
    xpad = jnp.concatenate([conv_state.astype(xbc.dtype), xbc], axis=1)
    w = p['conv_w'][l]
    conv = p['conv_b'][l] + sum(xpad[:, i:i + lq] * w[i] for i in range(CONV_W))
    conv = jax.nn.silu(conv)
    new_conv = xpad[:, lq:]
    xs, bm, cm = _split(conv, (D_INNER, SSD_GROUPS * D_STATE, SSD_GROUPS * D_STATE))
    xs = xs.reshape(bsz, lq, SSD_HEADS, SSD_HEAD_DIM).astype(F32)
    bm = jnp.repeat(bm.reshape(bsz, lq, SSD_GROUPS, D_STATE), HEADS_PER_GROUP, axis=2).astype(F32)
    cm = jnp.repeat(cm.reshape(bsz, lq, SSD_GROUPS, D_STATE), HEADS_PER_GROUP, axis=2).astype(F32)
    dt = jax.nn.softplus(dt_raw.astype(F32) + p['dt_bias'][l].astype(F32))
    a_neg = -jnp.exp(p['a_log'][l].astype(F32))
    if ssm_state is None:
        h0 = jnp.zeros((bsz, SSD_HEADS, SSD_HEAD_DIM, D_STATE), F32)
    else:
        h0 = ssm_state.astype(F32)
    chunk = SSD_CHUNK if lq % SSD_CHUNK == 0 else lq
    y, h_fin = _ssd_scan(xs, dt, a_neg, bm, cm, h0, chunk)
    y = y + p['d_skip'][l].astype(F32)[:, None] * xs
    y = y.reshape(bsz, lq, D_INNER)
    yz = (y * jax.nn.silu(z.astype(F32))).reshape(bsz, lq, SSD_GROUPS, D_INNER // SSD_GROUPS)
    yz = yz * lax.rsqrt(jnp.mean(yz * yz, axis=-1, keepdims=True) + EPS)
    out = (yz.reshape(bsz, lq, D_INNER) * p['norm_ssd'][l].astype(F32)).astype(z.dtype)
    return out, new_conv, h_fin


def _token_mix(h, pos0, past, p, l):
    bsz, lq = h.shape[:2]
    q, k, v, iq, ik, iw, z, xbc, dt_raw, ga, gb = _split(h @ p['w_in'][l], IN_SPLITS)
    pos = pos0 + jnp.arange(lq, dtype=jnp.int32)
    q = _rope(q.reshape(bsz, lq, N_HEADS, HEAD_DIM), pos)
    k = _rope(k.reshape(bsz, lq, KV_HEADS, HEAD_DIM), pos)
    v = v.reshape(bsz, lq, KV_HEADS, HEAD_DIM)
    iq = _rope_partial(iq.reshape(bsz, lq, IDX_HEADS, IDX_DIM), pos)
    ik = _rope_partial(ik[:, :, None, :], pos)[:, :, 0, :]
    iw = iw * IDX_SCALE
    if past is None:
        k_all, v_all, ik_all = k, v, ik
        conv_state, ssm_state = None, None
    else:
        k_past, v_past, ik_past, conv_state, ssm_state = past
        k_all = jnp.concatenate([k_past.astype(k.dtype), k], axis=1)
        v_all = jnp.concatenate([v_past.astype(v.dtype), v], axis=1)
        ik_all = jnp.concatenate([ik_past.astype(ik.dtype), ik], axis=1)
    n_top = min(INDEX_TOPK, k_all.shape[1] // 4)
    attn = _dsa_attention(q, iq, iw, pos, k_all, v_all, ik_all, n_top)
    ssd, new_conv, new_ssm = _ssd_branch(z, xbc, dt_raw, conv_state, ssm_state, p, l)
    merged = (jax.nn.sigmoid(ga) * (attn @ p['w_br_a'][l])
              + jax.nn.sigmoid(gb) * (ssd @ p['w_br_b'][l]))
    return merged @ p['w_out'][l], (k, v, ik, new_ssm, new_conv)


def _layer(x, c, pos0, past, p, l):
    mods = jax.nn.silu(c) @ p['w_ada'][l] + p['b_ada'][l]
    sh1, sc1, g1, sh2, sc2, g2, sh3, sc3, g3 = jnp.split(mods, N_MODS, axis=-1)
    h = _modulate(_rmsnorm(x, p['norm_ffn1'][l]), sh1, sc1)
    x = x + 0.5 * g1[:, None, :] * _swiglu(h, p['w_ffn1_up'][l], p['w_ffn1_down'][l])
    h = _modulate(_rmsnorm(x, p['norm_mix'][l]), sh2, sc2)
    m, state = _token_mix(h, pos0, past, p, l)
    x = x + g2[:, None, :] * m
    h = _modulate(_rmsnorm(x, p['norm_ffn2'][l]), sh3, sc3)
    x = x + 0.5 * g3[:, None, :] * _swiglu(h, p['w_ffn2_up'][l], p['w_ffn2_down'][l])
    return x, state


def setup_inputs(seed: int = 0) -> dict:
    key = jax.random.key(seed)
    ks = jax.random.split(key, 40)
    n_pages = PAST_LEN // PAGE_SIZE
    n_used = DEC_BATCH * n_pages
    n_pool = n_used + max(1, n_used // 4)

    def nrm(k, shape, scale):
        return jax.random.normal(k, shape, F32) * scale

    def gain(k, shape):
        return 1.0 + 0.01 * jax.random.normal(k, shape, F32)

    dt0 = jnp.exp(jax.random.uniform(ks[20], (DEPTH, SSD_HEADS), F32, math.log(1e-3), math.log(1e-1)))
    dt_bias = dt0 + jnp.log(-jnp.expm1(-dt0))
    a_log = jnp.log(jax.random.uniform(ks[21], (DEPTH, SSD_HEADS), F32, 1.0, 16.0))
    page_table = jax.random.permutation(ks[9], n_pool)[:n_used].reshape(DEC_BATCH, n_pages).astype(jnp.int32)
    return {
        'x_prompt': nrm(ks[0], (BATCH, SEQ, D_MODEL), 1.0),
        'x_sample': nrm(ks[1], (DEC_BATCH, DEC_SEQ, D_MODEL), 1.0),
        'cache_k': nrm(ks[2], (DEPTH, n_pool, PAGE_SIZE, KV_HEADS, HEAD_DIM), 1.0),
        'cache_v': nrm(ks[3], (DEPTH, n_pool, PAGE_SIZE, KV_HEADS, HEAD_DIM), 1.0),
        'cache_idx_k': nrm(ks[4], (DEPTH, n_pool, PAGE_SIZE, IDX_DIM), 1.0),
        'state_ssm': nrm(ks[5], (DEPTH, DEC_BATCH, SSD_HEADS, SSD_HEAD_DIM, D_STATE), 0.1),
        'state_conv': nrm(ks[6], (DEPTH, DEC_BATCH, CONV_W - 1, CONV_CH), 1.0),
        'page_table': page_table,
        'c_prompt': nrm(ks[7], (BATCH, D_MODEL), 1.0),
        'c_sample': nrm(ks[8], (DEC_BATCH, D_MODEL), 1.0),
        'w_ada': nrm(ks[10], (DEPTH, D_MODEL, N_MODS * D_MODEL), D_MODEL ** -0.5),
        'b_ada': nrm(ks[11], (DEPTH, N_MODS * D_MODEL), 0.01),
        'norm_ffn1': gain(ks[12], (DEPTH, D_MODEL)),
        'w_ffn1_up': nrm(ks[13], (DEPTH, D_MODEL, 2 * FFN_DIM), D_MODEL ** -0.5),
        'w_ffn1_down': nrm(ks[14], (DEPTH, FFN_DIM, D_MODEL), FFN_DIM ** -0.5),
        'norm_mix': gain(ks[15], (DEPTH, D_MODEL)),
        'w_in': nrm(ks[16], (DEPTH, D_MODEL, IN_DIM), D_MODEL ** -0.5),
        'conv_w': nrm(ks[17], (DEPTH, CONV_W, CONV_CH), CONV_W ** -0.5),
        'conv_b': nrm(ks[18], (DEPTH, CONV_CH), 0.01),
        'dt_bias': dt_bias,
        'a_log': a_log,
        'd_skip': gain(ks[22], (DEPTH, SSD_HEADS)),
        'norm_ssd': gain(ks[23], (DEPTH, D_INNER)),
        'w_br_a': nrm(ks[24], (DEPTH, ATT_DIM, D_MODEL), ATT_DIM ** -0.5),
        'w_br_b': nrm(ks[25], (DEPTH, D_INNER, D_MODEL), D_INNER ** -0.5),
        'w_out': nrm(ks[26], (DEPTH, D_MODEL, D_MODEL), D_MODEL ** -0.5),
        'norm_ffn2': gain(ks[27], (DEPTH, D_MODEL)),
        'w_ffn2_up': nrm(ks[28], (DEPTH, D_MODEL, 2 * FFN_DIM), D_MODEL ** -0.5),
        'w_ffn2_down': nrm(ks[29], (DEPTH, FFN_DIM, D_MODEL), FFN_DIM ** -0.5),
        'norm_final': gain(ks[30], (D_MODEL,)),
    }


def reference(x_prompt, x_sample, cache_k, cache_v, cache_idx_k, state_ssm, state_conv, page_table,
              c_prompt, c_sample, w_ada, b_ada, norm_ffn1, w_ffn1_up, w_ffn1_down, norm_mix, w_in,
              conv_w, conv_b, dt_bias, a_log, d_skip, norm_ssd, w_br_a, w_br_b, w_out,
              norm_ffn2, w_ffn2_up, w_ffn2_down, norm_final):
    p = {'w_ada': w_ada, 'b_ada': b_ada, 'norm_ffn1': norm_ffn1, 'w_ffn1_up': w_ffn1_up,
         'w_ffn1_down': w_ffn1_down, 'norm_mix': norm_mix, 'w_in': w_in, 'conv_w': conv_w,
         'conv_b': conv_b, 'dt_bias': dt_bias, 'a_log': a_log, 'd_skip': d_skip,
         'norm_ssd': norm_ssd, 'w_br_a': w_br_a, 'w_br_b': w_br_b, 'w_out': w_out,
         'norm_ffn2': norm_ffn2, 'w_ffn2_up': w_ffn2_up, 'w_ffn2_down': w_ffn2_down}
    xp, xs = x_prompt, x_sample
    kp, vp, ikp, sp, cp = [], [], [], [], []
    ksm, vsm, iks, ss, cs = [], [], [], [], []
    for l in range(DEPTH):
        xp, (k_n, v_n, ik_n, ssm_n, conv_n) = _layer(xp, c_prompt, 0, None, p, l)
        kp.append(k_n); vp.append(v_n); ikp.append(ik_n); sp.append(ssm_n); cp.append(conv_n)
        past = (_gather_pages(cache_k[l], page_table), _gather_pages(cache_v[l], page_table),
                _gather_pages(cache_idx_k[l], page_table), state_conv[l], state_ssm[l])
        xs, (k_n, v_n, ik_n, ssm_n, conv_n) = _layer(xs, c_sample, PAST_LEN, past, p, l)
        ksm.append(k_n); vsm.append(v_n); iks.append(ik_n); ss.append(ssm_n); cs.append(conv_n)
    y_prompt = _rmsnorm(xp, norm_final)
    y_sample = _rmsnorm(xs, norm_final)
    return (y_prompt, y_sample,
            jnp.stack(kp), jnp.stack(vp), jnp.stack(ikp), jnp.stack(sp), jnp.stack(cp),
            jnp.stack(ksm), jnp.stack(vsm), jnp.stack(iks), jnp.stack(ss), jnp.stack(cs))
```

```python
import functools
import math

import jax
import jax.numpy as jnp
from jax import lax
from jax.experimental import pallas as pl
from jax.experimental.pallas import tpu as pltpu

F32 = jnp.float32
BF16 = jnp.bfloat16
I32 = jnp.int32

D_MODEL = 1024
PAGE_SIZE = 128
N_HEADS = 8
HEAD_DIM = 128
KV_HEADS = 4
Q_PER_KV = N_HEADS // KV_HEADS
ATT_DIM = N_HEADS * HEAD_DIM
KV_DIM = KV_HEADS * HEAD_DIM
ATT_SCALE = HEAD_DIM ** -0.5
IDX_HEADS = 8
IDX_DIM = 64
IDX_ROPE_DIM = 32
IDX_SCALE = IDX_HEADS ** -0.5 * IDX_DIM ** -0.5
INDEX_TOPK = 256
ROPE_THETA = 10000.0
D_INNER = 2 * D_MODEL
SSD_HEAD_DIM = 64
SSD_HEADS = D_INNER // SSD_HEAD_DIM
SSD_GROUPS = 4
D_STATE = 128
CONV_W = 4
BC_DIM = SSD_GROUPS * D_STATE
CONV_CH = D_INNER + 2 * BC_DIM
SSD_CHUNK = 128
FFN_DIM = 2816
N_MODS = 9
EPS = 1e-6

LANES = 128
NEG_BIG = -1e30
INT_MIN = -(2 ** 31)
NEG_INF_KEY = -2139095041
VMEM_LIMIT = 56 * 1024 * 1024

A_XS, A_Z, A_GA, A_GB, A_BM, A_CM = 0, 2048, 4096, 5120, 6144, 6656
A_DIM = 7168
B_Q, B_K, B_V, B_IQ, B_SM = 0, 1024, 1536, 2048, 2560
B_DIM = 3072
SM_IW = IDX_DIM
SM_DT = IDX_DIM + IDX_HEADS
PROJ_TN = 512


def _cparams(*sem):
    return pltpu.CompilerParams(dimension_semantics=sem, vmem_limit_bytes=VMEM_LIMIT)


def _silu(v):
    return v * jax.nn.sigmoid(v)


def _norm_mod(x, g, shift, scale):
    xn = x * lax.rsqrt(jnp.mean(x * x, axis=-1, keepdims=True) + EPS) * g
    return xn * (1.0 + scale) + shift


def _dot(a, b):
    return jnp.dot(a, b, preferred_element_type=F32)


def _dot_nt(a, b):
    return lax.dot_general(a, b, (((1,), (1,)), ((), ())), preferred_element_type=F32)


def _ada_kernel(c_ref, w_ref, b_ref, o_ref):
    s = _silu(c_ref[...]).astype(BF16)
    o_ref[0] = _dot(s, w_ref[0].astype(BF16)) + b_ref[0]


def _ada(c_all, w_ada, b_ada):
    depth, d, n = w_ada.shape
    r = c_all.shape[0]
    tn = 1024
    return pl.pallas_call(
        _ada_kernel,
        grid=(depth, n // tn),
        in_specs=[
            pl.BlockSpec((r, d), lambda l, j: (0, 0)),
            pl.BlockSpec((1, d, tn), lambda l, j: (l, 0, j)),
            pl.BlockSpec((1, 1, tn), lambda l, j: (l, 0, j)),
        ],
        out_specs=pl.BlockSpec((1, r, tn), lambda l, j: (l, 0, j)),
        out_shape=jax.ShapeDtypeStruct((depth, r, n), F32),
        compiler_params=_cparams("parallel", "parallel"),
        name="adaln",
    )(c_all, w_ada, b_ada.reshape(depth, 1, n))


def _ffn_kernel(x_ref, sh_ref, sc_ref, gt_ref, g_ref, gf_ref, wg_ref, wu_ref, wd_ref,
                o_ref, h_sc, acc_sc, *, final_norm):
    j = pl.program_id(1)

    @pl.when(j == 0)
    def _():
        h_sc[...] = _norm_mod(x_ref[...], g_ref[...], sh_ref[0], sc_ref[0]).astype(BF16)
        acc_sc[...] = jnp.zeros_like(acc_sc)

    h = h_sc[...]
    a = _dot(h, wg_ref[...])
    b = _dot(h, wu_ref[...])
    t = (_silu(a) * b).astype(BF16)
    acc_sc[...] += _dot(t, wd_ref[...])

    @pl.when(j == pl.num_programs(1) - 1)
    def _():
        y = x_ref[...] + 0.5 * gt_ref[0] * acc_sc[...]
        if final_norm:
            y = y * lax.rsqrt(jnp.mean(y * y, axis=-1, keepdims=True) + EPS) * gf_ref[...]
        o_ref[...] = y


def _ffn(x, sh, sc, gt, g, gf, w_up, w_down, *, tm, final_norm):
    t, d = x.shape
    f = w_down.shape[0]
    tf = 256
    nf = f // tf
    groups, r, _ = sh.shape
    tpg = (t // groups) // tm
    mod_spec = pl.BlockSpec((1, r, d), lambda i, j: (i // tpg, 0, 0))
    return pl.pallas_call(
        functools.partial(_ffn_kernel, final_norm=final_norm),
        grid=(t // tm, nf),
        in_specs=[
            pl.BlockSpec((tm, d), lambda i, j: (i, 0)),
            mod_spec, mod_spec, mod_spec,
            pl.BlockSpec((1, d), lambda i, j: (0, 0)),
            pl.BlockSpec((1, d), lambda i, j: (0, 0)),
            pl.BlockSpec((d, tf), lambda i, j: (0, j)),
            pl.BlockSpec((d, tf), lambda i, j: (0, j + nf)),
            pl.BlockSpec((tf, d), lambda i, j: (j, 0)),
        ],
        out_specs=pl.BlockSpec((tm, d), lambda i, j: (i, 0)),
        out_shape=jax.ShapeDtypeStruct((t, d), F32),
        scratch_shapes=[pltpu.VMEM((tm, d), BF16), pltpu.VMEM((tm, d), F32)],
        compiler_params=_cparams("parallel", "arbitrary"),
        name="ffn",
    )(x, sh, sc, gt, g, gf, w_up, w_up, w_down)


def _proj_a_kernel(x_ref, sh_ref, sc_ref, g_ref, w_ref, o_ref, h_sc):
    @pl.when(pl.program_id(1) == 0)
    def _():
        h_sc[...] = _norm_mod(x_ref[...], g_ref[...], sh_ref[0], sc_ref[0]).astype(BF16)

    o_ref[...] = _dot(h_sc[...], w_ref[...])


def _proj_b_kernel(x_ref, sh_ref, sc_ref, g_ref, w_ref, t64_ref, t16_ref, o_ref, ob_ref, h_sc):
    j = pl.program_id(1)

    @pl.when(j == 0)
    def _():
        h_sc[...] = _norm_mod(x_ref[...], g_ref[...], sh_ref[0], sc_ref[0]).astype(BF16)

    y = _dot(h_sc[...], w_ref[...])
    groups = y.shape[1] // LANES

    def emit(g, v):
        o_ref[:, g * LANES:(g + 1) * LANES] = v
        ob_ref[:, g * LANES:(g + 1) * LANES] = v.astype(BF16)

    @pl.when(j < B_V // PROJ_TN)
    def _():
        c = t64_ref[:, 0:LANES]
        a = t64_ref[:, LANES:2 * LANES]
        for g in range(groups):
            yg = y[:, g * LANES:(g + 1) * LANES]
            emit(g, yg * c + pltpu.roll(yg, HEAD_DIM // 2, 1) * a)

    @pl.when(j == B_V // PROJ_TN)
    def _():
        for g in range(groups):
            emit(g, y[:, g * LANES:(g + 1) * LANES])

    @pl.when(j > B_V // PROJ_TN)
    def _():
        half = IDX_ROPE_DIM // 2
        c = t16_ref[0, :, 0:LANES]
        a = t16_ref[0, :, LANES:2 * LANES]
        b = t16_ref[0, :, 2 * LANES:3 * LANES]
        for g in range(groups):
            yg = y[:, g * LANES:(g + 1) * LANES]
            emit(g, yg * c + pltpu.roll(yg, half, 1) * a + pltpu.roll(yg, LANES - half, 1) * b)


def _proj_a(x, sh, sc, g, w, *, tm):
    t, d = x.shape
    n = w.shape[1]
    groups, r, _ = sh.shape
    tpg = (t // groups) // tm
    mod_spec = pl.BlockSpec((1, r, d), lambda i, j: (i // tpg, 0, 0))
    return pl.pallas_call(
        _proj_a_kernel,
        grid=(t // tm, n // PROJ_TN),
        in_specs=[
            pl.BlockSpec((tm, d), lambda i, j: (i, 0)),
            mod_spec, mod_spec,
            pl.BlockSpec((1, d), lambda i, j: (0, 0)),
            pl.BlockSpec((d, PROJ_TN), lambda i, j: (0, j)),
        ],
        out_specs=pl.BlockSpec((tm, PROJ_TN), lambda i, j: (i, j)),
        out_shape=jax.ShapeDtypeStruct((t, n), F32),
        scratch_shapes=[pltpu.VMEM((tm, d), BF16)],
        compiler_params=_cparams("parallel", "arbitrary"),
        name="proj_a",
    )(x, sh, sc, g, w)


def _proj_b(x, sh, sc, g, w, t64, t16, *, tm):
    t, d = x.shape
    n = w.shape[1]
    groups, r, _ = sh.shape
    tpg = (t // groups) // tm
    npos = t64.shape[0] // tm
    mod_spec = pl.BlockSpec((1, r, d), lambda i, j: (i // tpg, 0, 0))
    sm_tile = B_SM // PROJ_TN
    return pl.pallas_call(
        _proj_b_kernel,
        grid=(t // tm, n // PROJ_TN),
        in_specs=[
            pl.BlockSpec((tm, d), lambda i, j: (i, 0)),
            mod_spec, mod_spec,
            pl.BlockSpec((1, d), lambda i, j: (0, 0)),
            pl.BlockSpec((d, PROJ_TN), lambda i, j: (0, j)),
            pl.BlockSpec((tm, 2 * LANES), lambda i, j: (i % npos, 0)),
            pl.BlockSpec((1, tm, 3 * LANES),
                         lambda i, j: (jnp.where(j == sm_tile, 1, 0), i % npos, 0)),
        ],
        out_specs=[pl.BlockSpec((tm, PROJ_TN), lambda i, j: (i, j)),
                   pl.BlockSpec((tm, PROJ_TN), lambda i, j: (i, j))],
        out_shape=[jax.ShapeDtypeStruct((t, n), F32), jax.ShapeDtypeStruct((t, n), BF16)],
        scratch_shapes=[pltpu.VMEM((tm, d), BF16)],
        compiler_params=_cparams("parallel", "arbitrary"),
        name="proj_b",
    )(x, sh, sc, g, w, t64, t16)


def _rope_tables(pos):
    posf = pos.astype(F32)[:, None]
    half = HEAD_DIM // 2
    inv = ROPE_THETA ** (-jnp.arange(half, dtype=F32) / half)
    ang = posf * inv[None, :]
    cos, sin = jnp.cos(ang), jnp.sin(ang)
    t64 = jnp.concatenate([cos, cos, -sin, sin], axis=1)
    ih = IDX_ROPE_DIM // 2
    inv_i = ROPE_THETA ** (-jnp.arange(ih, dtype=F32) / ih)
    ang_i = posf * inv_i[None, :]
    ci, si = jnp.cos(ang_i), jnp.sin(ang_i)
    n = pos.shape[0]
    ones = jnp.ones((n, IDX_DIM - IDX_ROPE_DIM), F32)
    zeros = jnp.zeros((n, IDX_DIM - IDX_ROPE_DIM), F32)
    zh = jnp.zeros((n, ih), F32)
    c_head = jnp.concatenate([ci, ci, ones], axis=1)
    a_head = jnp.concatenate([zh, si, zeros], axis=1)
    b_head = jnp.concatenate([-si, zh, zeros], axis=1)
    t_iq = jnp.concatenate([c_head, c_head, a_head, a_head, b_head, b_head], axis=1)
    z64 = jnp.zeros((n, IDX_DIM), F32)
    c_tail = jnp.concatenate([jnp.full((n, IDX_HEADS), IDX_SCALE, F32),
                              jnp.ones((n, LANES - IDX_DIM - IDX_HEADS), F32)], axis=1)
    t_sm = jnp.concatenate([c_head, c_tail, a_head, z64, b_head, z64], axis=1)
    return t64, jnp.stack([t_iq, t_sm])


def _sortable(x):
    b = lax.bitcast_convert_type(x, I32)
    b = jnp.where(b == INT_MIN, 0, b)
    return jnp.where(b < 0, b ^ 0x7FFFFFFF, b)


def _select_topk(key_sc, bias_sc, j_sc, nk, n_top, idx_bits):
    _, rows, ck = key_sc.shape

    def count(pred):
        def body(c, acc):
            m = pred(key_sc[c], c).astype(I32)
            for t in range(ck // LANES):
                acc = acc + m[:, t * LANES:(t + 1) * LANES]
            return acc
        acc = lax.fori_loop(0, nk, body, jnp.zeros((rows, LANES), I32))
        return jnp.sum(acc, axis=1, keepdims=True)

    def vstep(it, lo):
        cand = lo + lax.shift_left(jnp.int32(1), 31 - it)
        cnt = count(lambda k, c: k >= cand)
        return jnp.where(cnt >= n_top, cand, lo)

    lo = lax.fori_loop(0, 32, vstep, jnp.full((rows, 1), INT_MIN, I32))
    cnt_gt = count(lambda k, c: k > lo)
    cnt_ge = count(lambda k, c: k >= lo)
    need = n_top - cnt_gt
    excess = (cnt_ge > n_top) & (lo > NEG_INF_KEY)
    big = jnp.int32(2 ** 30)
    j_sc[...] = jnp.full(j_sc.shape, big, I32)

    def col_index(c):
        return lax.broadcasted_iota(I32, (rows, ck), 1) + c * ck

    @pl.when(jnp.max(excess.astype(I32)) > 0)
    def _():
        def jstep(it, m):
            cand = m + lax.shift_left(jnp.int32(1), idx_bits - 1 - it)
            f = count(lambda k, c: (k == lo) & (col_index(c) < cand))
            return jnp.where(f < need, cand, m)
        m = lax.fori_loop(0, idx_bits, jstep, jnp.zeros((rows, 1), I32))
        j_sc[...] = jnp.broadcast_to(jnp.where(excess, m, big), j_sc.shape)

    jmax = j_sc[:, 0:1]

    def wbody(c, carry):
        k = key_sc[c]
        sel = (k > lo) | ((k == lo) & (col_index(c) <= jmax))
        sel = sel & (k > NEG_INF_KEY)
        bias_sc[c] = jnp.where(sel, 0.0, NEG_BIG)
        return carry

    lax.fori_loop(0, nk, wbody, 0)


def _flash_step(s, v, m_prev, l_prev, acc_prev):
    m_new = jnp.maximum(m_prev, jnp.max(s, axis=1, keepdims=True))
    alpha = jnp.exp(m_prev - m_new)
    p = jnp.exp(s - m_new)
    l_new = alpha * l_prev + jnp.sum(p, axis=1, keepdims=True)
    acc_new = alpha * acc_prev + _dot(p.astype(BF16), v)
    return m_new, l_new, acc_new


ATT_TQ = 128
ATT_CK = 512


def _dsa_prompt_kernel(q_ref, iq_ref, smq_ref, k_ref, v_ref, smk_ref, o_ref,
                       key_sc, bias_sc, j_sc, *, n_top, idx_bits):
    qi = pl.program_id(1)
    tq, ck = ATT_TQ, ATT_CK
    nk = (qi * tq + tq + ck - 1) // ck
    lane = lax.broadcasted_iota(I32, (tq, LANES), 1)

    iq = iq_ref[...].astype(F32)
    iqh = []
    for p in range(IDX_HEADS // 2):
        grp = iq[:, p * LANES:(p + 1) * LANES]
        iqh.append(jnp.where(lane < IDX_DIM, grp, 0.0).astype(BF16))
        iqh.append(jnp.where(lane < IDX_DIM, pltpu.roll(grp, IDX_DIM, 1), 0.0).astype(BF16))
    smq = smq_ref[...]
    qpos = qi * tq + lax.broadcasted_iota(I32, (tq, ck), 0)

    def score_body(c, carry):
        kc = smk_ref[pl.ds(pl.multiple_of(c * ck, ck), ck), :]
        sc = jnp.zeros((tq, ck), F32)
        for h in range(IDX_HEADS):
            lg = _dot_nt(iqh[h], kc)
            sc = sc + jnp.maximum(lg, 0.0) * smq[:, SM_IW + h:SM_IW + h + 1]
        kpos = c * ck + lax.broadcasted_iota(I32, (tq, ck), 1)
        sc = jnp.where(kpos <= qpos, sc, -jnp.inf)
        key_sc[c] = _sortable(sc)
        return carry

    lax.fori_loop(0, nk, score_body, 0)
    _select_topk(key_sc, bias_sc, j_sc, nk, n_top, idx_bits)

    for g in range(KV_HEADS):
        q2 = jnp.concatenate(
            [q_ref[:, (2 * g) * HEAD_DIM:(2 * g + 1) * HEAD_DIM],
             q_ref[:, (2 * g + 1) * HEAD_DIM:(2 * g + 2) * HEAD_DIM]], axis=0)

        def att_body(c, carry):
            m_prev, l_prev, acc_prev = carry
            off = pl.multiple_of(c * ck, ck)
            kc = k_ref[pl.ds(off, ck), g * HEAD_DIM:(g + 1) * HEAD_DIM]
            vc = v_ref[pl.ds(off, ck), g * HEAD_DIM:(g + 1) * HEAD_DIM]
            bias = bias_sc[c]
            s = _dot_nt(q2, kc) * ATT_SCALE + jnp.concatenate([bias, bias], axis=0)
            return _flash_step(s, vc, m_prev, l_prev, acc_prev)

        init = (jnp.full((2 * tq, 1), NEG_BIG, F32), jnp.zeros((2 * tq, 1), F32),
                jnp.zeros((2 * tq, HEAD_DIM), F32))
        _, l_fin, acc = lax.fori_loop(0, nk, att_body, init)
        o = acc / l_fin
        o_ref[:, (2 * g) * HEAD_DIM:(2 * g + 1) * HEAD_DIM] = o[:tq].astype(o_ref.dtype)
        o_ref[:, (2 * g + 1) * HEAD_DIM:(2 * g + 2) * HEAD_DIM] = o[tq:].astype(o_ref.dtype)


def _dsa_prompt(yb, ybh, bsz, seq, n_top):
    nq = seq // ATT_TQ
    nch = seq // ATT_CK
    kernel = functools.partial(_dsa_prompt_kernel, n_top=n_top,
                               idx_bits=max(1, (seq - 1).bit_length()))
    return pl.pallas_call(
        kernel,
        grid=(bsz, nq),
        in_specs=[
            pl.BlockSpec((ATT_TQ, ATT_DIM), lambda b, i: (b * nq + i, B_Q // ATT_DIM)),
            pl.BlockSpec((ATT_TQ, 512), lambda b, i: (b * nq + i, B_IQ // 512)),
            pl.BlockSpec((ATT_TQ, LANES), lambda b, i: (b * nq + i, B_SM // LANES)),
            pl.BlockSpec((seq, KV_DIM), lambda b, i: (b, B_K // KV_DIM)),
            pl.BlockSpec((seq, KV_DIM), lambda b, i: (b, B_V // KV_DIM)),
            pl.BlockSpec((seq, LANES), lambda b, i: (b, B_SM // LANES)),
        ],
        out_specs=pl.BlockSpec((ATT_TQ, ATT_DIM), lambda b, i: (b * nq + i, 0)),
        out_shape=jax.ShapeDtypeStruct((bsz * seq, ATT_DIM), BF16),
        scratch_shapes=[pltpu.VMEM((nch, ATT_TQ, ATT_CK), I32),
                        pltpu.VMEM((nch, ATT_TQ, ATT_CK), F32),
                        pltpu.VMEM((ATT_TQ, LANES), I32)],
        compiler_params=_cparams("parallel", "arbitrary"),
        name="dsa_prompt",
    )(ybh, ybh, yb, ybh, ybh, ybh)


PAGES_PER_STEP = 8


def _sample_score_kernel(pt_ref, iq_ref, sm_ref, smn_ref, *refs):
    del pt_ref
    npg = PAGES_PER_STEP
    ik_refs = refs[:npg]
    sp_ref, sn_ref = refs[npg], refs[npg + 1]
    lq = iq_ref.shape[1]
    iq = iq_ref[0]
    sm = sm_ref[0]
    iqh = jnp.concatenate([iq[:, h * IDX_DIM:(h + 1) * IDX_DIM] for h in range(IDX_HEADS)],
                          axis=0).astype(BF16)

    def score(keys):
        lg = _dot_nt(iqh, keys.astype(BF16))
        sc = jnp.zeros((lq, keys.shape[0]), F32)
        for h in range(IDX_HEADS):
            sc = sc + jnp.maximum(lg[h * lq:(h + 1) * lq], 0.0) * sm[:, SM_IW + h:SM_IW + h + 1]
        return sc

    for i in range(npg):
        sp_ref[0, :, i * PAGE_SIZE:(i + 1) * PAGE_SIZE] = score(ik_refs[i][0])

    @pl.when(pl.program_id(1) == 0)
    def _():
        sc = score(smn_ref[0][:, 0:IDX_DIM])
        col = lax.broadcasted_iota(I32, sc.shape, 1)
        row = lax.broadcasted_iota(I32, sc.shape, 0)
        sn_ref[0] = jnp.where(col <= row, sc, -jnp.inf)


def _sample_scores(page_table, yb3, sm_pad, pool_ik):
    bsz, lq, _ = yb3.shape
    n_pages = page_table.shape[1]
    npg = PAGES_PER_STEP
    steps = n_pages // npg
    ik_specs = [pl.BlockSpec((1, PAGE_SIZE, IDX_DIM),
                             functools.partial(lambda b, j, pt, i: (pt[b, j * npg + i], 0, 0), i=i))
                for i in range(npg)]
    grid_spec = pltpu.PrefetchScalarGridSpec(
        num_scalar_prefetch=1,
        grid=(bsz, steps),
        in_specs=[
            pl.BlockSpec((1, lq, 512), lambda b, j, pt: (b, 0, B_IQ // 512)),
            pl.BlockSpec((1, lq, LANES), lambda b, j, pt: (b, 0, B_SM // LANES)),
            pl.BlockSpec((1, PAGE_SIZE, LANES), lambda b, j, pt: (b, 0, 0)),
        ] + ik_specs,
        out_specs=[pl.BlockSpec((1, lq, npg * PAGE_SIZE), lambda b, j, pt: (b, 0, j)),
                   pl.BlockSpec((1, lq, PAGE_SIZE), lambda b, j, pt: (b, 0, 0))],
    )
    return pl.pallas_call(
        _sample_score_kernel,
        grid_spec=grid_spec,
        out_shape=[jax.ShapeDtypeStruct((bsz, lq, n_pages * PAGE_SIZE), F32),
                   jax.ShapeDtypeStruct((bsz, lq, PAGE_SIZE), F32)],
        compiler_params=_cparams("parallel", "arbitrary"),
        name="sample_scores",
    )(page_table, yb3, yb3, sm_pad, *([pool_ik] * npg))


SEL_CK = 512


def _sample_select_kernel(s_ref, o_ref, key_sc, bias_sc, j_sc, *, n_top, idx_bits):
    nch = key_sc.shape[0]
    for c in range(nch):
        key_sc[c] = _sortable(s_ref[:, c * SEL_CK:(c + 1) * SEL_CK])
    _select_topk(key_sc, bias_sc, j_sc, nch, n_top, idx_bits)
    for c in range(nch):
        o_ref[:, c * SEL_CK:(c + 1) * SEL_CK] = bias_sc[c]


def _sample_select(scores, n_top):
    rows, width = scores.shape
    nch = width // SEL_CK
    kernel = functools.partial(_sample_select_kernel, n_top=n_top,
                               idx_bits=max(1, (width - 1).bit_length()))
    return pl.pallas_call(
        kernel,
        grid=(1,),
        in_specs=[pl.BlockSpec((rows, width), lambda i: (0, 0))],
        out_specs=pl.BlockSpec((rows, width), lambda i: (0, 0)),
        out_shape=jax.ShapeDtypeStruct((rows, width), F32),
        scratch_shapes=[pltpu.VMEM((nch, rows, SEL_CK), I32),
                        pltpu.VMEM((nch, rows, SEL_CK), F32),
                        pltpu.VMEM((rows, LANES), I32)],
        compiler_params=_cparams("arbitrary"),
        name="sample_select",
    )(scores)


def _sample_attn_kernel(pt_ref, q_ref, bias_ref, biasn_ref, kn_ref, vn_ref, *refs,):
    del pt_ref
    npg = PAGES_PER_STEP
    k_refs = refs[:npg]
    v_refs = refs[npg:2 * npg]
    o_ref, m_sc, l_sc, acc_sc = refs[2 * npg:]
    j = pl.program_id(1)
    lq = q_ref.shape[1]
    rows = Q_PER_KV * lq

    @pl.when(j == 0)
    def _():
        m_sc[...] = jnp.full(m_sc.shape, NEG_BIG, F32)
        l_sc[...] = jnp.zeros_like(l_sc)
        acc_sc[...] = jnp.zeros_like(acc_sc)

    q = q_ref[0]

    def q_group(g):
        return jnp.concatenate(
            [q[:, (Q_PER_KV * g + t) * HEAD_DIM:(Q_PER_KV * g + t + 1) * HEAD_DIM]
             for t in range(Q_PER_KV)], axis=0).astype(BF16)

    def accumulate(g, keys, vals, bias):
        s = _dot_nt(q_group(g), keys) * ATT_SCALE + jnp.concatenate([bias] * Q_PER_KV, axis=0)
        m_new, l_new, acc_new = _flash_step(
            s, vals, m_sc[g][:, 0:1], l_sc[g][:, 0:1], acc_sc[g])
        m_sc[g] = jnp.broadcast_to(m_new, (rows, LANES))
        l_sc[g] = jnp.broadcast_to(l_new, (rows, LANES))
        acc_sc[g] = acc_new

    bias = bias_ref[0]
    for g in range(KV_HEADS):
        sl = slice(g * HEAD_DIM, (g + 1) * HEAD_DIM)
        keys = jnp.concatenate([k_refs[i][0][:, sl] for i in range(npg)], axis=0).astype(BF16)
        vals = jnp.concatenate([v_refs[i][0][:, sl] for i in range(npg)], axis=0).astype(BF16)
        accumulate(g, keys, vals, bias)

    @pl.when(j == pl.num_programs(1) - 1)
    def _():
        bias_n = biasn_ref[0]
        for g in range(KV_HEADS):
            sl = slice(g * HEAD_DIM, (g + 1) * HEAD_DIM)
            accumulate(g, kn_ref[0][:, sl].astype(BF16), vn_ref[0][:, sl].astype(BF16), bias_n)
            o = acc_sc[g] / l_sc[g][:, 0:1]
            for t in range(Q_PER_KV):
                h = Q_PER_KV * g + t
                o_ref[0, :, h * HEAD_DIM:(h + 1) * HEAD_DIM] = o[t * lq:(t + 1) * lq].astype(o_ref.dtype)


def _sample_attn(page_table, yb3, bias3, kn_pad, vn_pad, pool_k, pool_v):
    bsz, lq, _ = yb3.shape
    n_pages = page_table.shape[1]
    npg = PAGES_PER_STEP
    steps = n_pages // npg
    past = n_pages * PAGE_SIZE

    def page_spec(i):
        return pl.BlockSpec((1, PAGE_SIZE, KV_DIM),
                            functools.partial(lambda b, j, pt, i: (pt[b, j * npg + i], 0, 0), i=i))

    grid_spec = pltpu.PrefetchScalarGridSpec(
        num_scalar_prefetch=1,
        grid=(bsz, steps),
        in_specs=[
            pl.BlockSpec((1, lq, ATT_DIM), lambda b, j, pt: (b, 0, B_Q // ATT_DIM)),
            pl.BlockSpec((1, lq, npg * PAGE_SIZE), lambda b, j, pt: (b, 0, j)),
            pl.BlockSpec((1, lq, PAGE_SIZE), lambda b, j, pt: (b, 0, past // PAGE_SIZE)),
            pl.BlockSpec((1, PAGE_SIZE, KV_DIM), lambda b, j, pt: (b, 0, 0)),
            pl.BlockSpec((1, PAGE_SIZE, KV_DIM), lambda b, j, pt: (b, 0, 0)),
        ] + [page_spec(i) for i in range(npg)] + [page_spec(i) for i in range(npg)],
        out_specs=pl.BlockSpec((1, lq, ATT_DIM), lambda b, j, pt: (b, 0, 0)),
        scratch_shapes=[pltpu.VMEM((KV_HEADS, Q_PER_KV * lq, LANES), F32),
                        pltpu.VMEM((KV_HEADS, Q_PER_KV * lq, LANES), F32),
                        pltpu.VMEM((KV_HEADS, Q_PER_KV * lq, HEAD_DIM), F32)],
    )
    return pl.pallas_call(
        _sample_attn_kernel,
        grid_spec=grid_spec,
        out_shape=jax.ShapeDtypeStruct((bsz, lq, ATT_DIM), BF16),
        compiler_params=_cparams("parallel", "arbitrary"),
        name="sample_attn",
    )(page_table, yb3, bias3, bias3, kn_pad, vn_pad, *([pool_k] * npg), *([pool_v] * npg))


def _ssd_kernel(xs_ref, z_ref, bm_ref, cm_ref, sm_ref, cs_ref, h0_ref, cw_ref, cb_ref,
                dtb_ref, aneg_ref, dsk_ref, nrm_ref, exp_ref,
                y_ref, hout_ref, xp_sc, sm_sc, ht_sc, *, rows, has_state, lead):
    c = pl.program_id(1)
    lc = SSD_CHUNK
    pad = 8
    hist = CONV_W - 1

    def blk(ref):
        return ref[0] if lead else ref[...]

    @pl.when(c == 0)
    def _():
        xp_sc[...] = jnp.zeros_like(xp_sc)
        sm_sc[...] = jnp.zeros_like(sm_sc)
        if has_state:
            xp_sc[pad - hist:pad, :] = cs_ref[0]
            for jp in range(SSD_HEADS // 2):
                pair = jnp.concatenate([h0_ref[0, 2 * jp], h0_ref[0, 2 * jp + 1]], axis=0)
                ht_sc[:, jp * LANES:(jp + 1) * LANES] = pair.T
        else:
            ht_sc[...] = jnp.zeros_like(ht_sc)

    @pl.when(c > 0)
    def _():
        xp_sc[0:pad, :] = xp_sc[lc:lc + pad, :]

    xp_sc[pad:pad + rows, 0:D_INNER] = blk(xs_ref)
    xp_sc[pad:pad + rows, D_INNER:D_INNER + BC_DIM] = blk(bm_ref)
    xp_sc[pad:pad + rows, D_INNER + BC_DIM:CONV_CH] = blk(cm_ref)
    sm_sc[0:rows, :] = blk(sm_ref)

    conv = cb_ref[...]
    for i in range(CONV_W):
        conv = conv + cw_ref[i:i + 1, :] * xp_sc[pad - hist + i:pad - hist + i + lc, :]
    conv = _silu(conv)
    xs = conv[:, 0:D_INNER]
    bmat = conv[:, D_INNER:D_INNER + BC_DIM]
    cmat = conv[:, D_INNER + BC_DIM:CONV_CH]

    lane = lax.broadcasted_iota(I32, (lc, LANES), 1)
    row = lax.broadcasted_iota(I32, (lc, LANES), 0)
    dt_lanes = (lane >= SM_DT) & (lane < SM_DT + SSD_HEADS) & (row < rows)
    raw = sm_sc[...] + dtb_ref[...]
    softplus = jnp.maximum(raw, 0.0) + jnp.log(1.0 + jnp.exp(-jnp.abs(raw)))
    dt = jnp.where(dt_lanes, softplus, 0.0)
    a = dt * aneg_ref[...]
    ltri = (row >= lane).astype(F32)
    acum = jnp.dot(ltri, a, preferred_element_type=F32, precision=lax.Precision.HIGHEST)
    acum_t = acum.T
    dt_t = dt.T
    a_last = acum[lc - 1:lc, :]
    w_end = jnp.exp(a_last - acum) * dt
    expand = exp_ref[...]
    w_exp = _dot(w_end.astype(BF16), expand)
    e_exp = _dot(jnp.exp(acum).astype(BF16), expand)
    cd_exp = jnp.dot(jnp.broadcast_to(jnp.exp(a_last), (8, LANES)), expand.astype(F32),
                     preferred_element_type=F32, precision=lax.Precision.HIGHEST)[0:1, :]
    xw = (xs * w_exp).astype(BF16)
    xs_b = xs.astype(BF16)
    causal = row >= lane

    ht_prev = ht_sc[...]
    gw = D_INNER // SSD_GROUPS
    heads_pg = SSD_HEADS // SSD_GROUPS
    y_parts = []
    for g in range(SSD_GROUPS):
        bg = bmat[:, g * D_STATE:(g + 1) * D_STATE]
        cg = cmat[:, g * D_STATE:(g + 1) * D_STATE].astype(BF16)
        cb = _dot_nt(cg, bg.astype(BF16))
        y_off = _dot(cg, ht_prev[:, g * gw:(g + 1) * gw].astype(BF16)) * e_exp[:, g * gw:(g + 1) * gw]
        states_t = _dot(bg.T.astype(BF16), xw[:, g * gw:(g + 1) * gw])
        ht_sc[:, g * gw:(g + 1) * gw] = (ht_prev[:, g * gw:(g + 1) * gw] * cd_exp[:, g * gw:(g + 1) * gw]
                                         + states_t)
        diag = []
        for jp in range(heads_pg // 2):
            ms = []
            for t in range(2):
                h = g * heads_pg + 2 * jp + t
                col = acum[:, SM_DT + h:SM_DT + h + 1]
                rowv = acum_t[SM_DT + h:SM_DT + h + 1, :]
                dec = jnp.exp(jnp.where(causal, col - rowv, NEG_BIG))
                ms.append((cb * dec * dt_t[SM_DT + h:SM_DT + h + 1, :]).astype(BF16))
            mpair = jnp.concatenate(ms, axis=1)
            xpair = xs_b[:, g * gw + jp * LANES:g * gw + (jp + 1) * LANES]
            zero = jnp.zeros_like(xpair)
            xbd = jnp.concatenate([jnp.where(lane < SSD_HEAD_DIM, xpair, zero),
                                   jnp.where(lane >= SSD_HEAD_DIM, xpair, zero)], axis=0)
            diag.append(_dot(mpair, xbd))
        y_parts.append(jnp.concatenate(diag, axis=1) + y_off)
    y = jnp.concatenate(y_parts, axis=1) + dsk_ref[...] * xs

    zval = blk(z_ref)
    if rows < lc:
        zval = jnp.concatenate([zval, jnp.zeros((lc - rows, D_INNER), F32)], axis=0)
    yz = y * _silu(zval)
    outs = []
    for g in range(SSD_GROUPS):
        part = yz[:, g * gw:(g + 1) * gw]
        part = part * lax.rsqrt(jnp.mean(part * part, axis=-1, keepdims=True) + EPS)
        outs.append(part * nrm_ref[:, g * gw:(g + 1) * gw])
    out = jnp.concatenate(outs, axis=1)
    if lead:
        y_ref[0] = out[0:rows].astype(y_ref.dtype)
    else:
        y_ref[...] = out.astype(y_ref.dtype)

    @pl.when(c == pl.num_programs(1) - 1)
    def _():
        for jp in range(SSD_HEADS // 2):
            pair = ht_sc[:, jp * LANES:(jp + 1) * LANES].T
            hout_ref[0, 2 * jp] = pair[0:SSD_HEAD_DIM]
            hout_ref[0, 2 * jp + 1] = pair[SSD_HEAD_DIM:]


def _ssd_consts(p, l):
    pad_l = SM_DT
    pad_r = LANES - SM_DT - SSD_HEADS

    def lanes(v):
        return jnp.pad(v.astype(F32), (pad_l, pad_r)).reshape(1, LANES)

    a_neg = -jnp.exp(p['a_log'][l].astype(F32))
    heads = jnp.arange(D_INNER, dtype=I32) // SSD_HEAD_DIM
    expand = (jnp.arange(LANES, dtype=I32)[:, None] == heads[None, :] + SM_DT).astype(BF16)
    return dict(
        cw=p['conv_w'][l], cb=p['conv_b'][l].reshape(1, CONV_CH),
        dtb=lanes(p['dt_bias'][l]), aneg=lanes(a_neg),
        dsk=jnp.repeat(p['d_skip'][l].astype(F32), SSD_HEAD_DIM).reshape(1, D_INNER),
        nrm=p['norm_ssd'][l].reshape(1, D_INNER), expand=expand)


def _ssd_prompt(ya, yb, bsz, seq, k):
    nc = seq // SSD_CHUNK
    lc = SSD_CHUNK
    const = lambda shape: pl.BlockSpec(shape, lambda b, c: (0,) * len(shape))
    dummy_cs = jnp.zeros((1, CONV_W - 1, CONV_CH), F32)
    dummy_h0 = jnp.zeros((1, 2, SSD_HEAD_DIM, D_STATE), F32)
    return pl.pallas_call(
        functools.partial(_ssd_kernel, rows=lc, has_state=False, lead=False),
        grid=(bsz, nc),
        in_specs=[
            pl.BlockSpec((lc, D_INNER), lambda b, c: (b * nc + c, A_XS // D_INNER)),
            pl.BlockSpec((lc, D_INNER), lambda b, c: (b * nc + c, A_Z // D_INNER)),
            pl.BlockSpec((lc, BC_DIM), lambda b, c: (b * nc + c, A_BM // BC_DIM)),
            pl.BlockSpec((lc, BC_DIM), lambda b, c: (b * nc + c, A_CM // BC_DIM)),
            pl.BlockSpec((lc, LANES), lambda b, c: (b * nc + c, B_SM // LANES)),
            const((1, CONV_W - 1, CONV_CH)),
            const((1, 2, SSD_HEAD_DIM, D_STATE)),
            const((CONV_W, CONV_CH)), const((1, CONV_CH)),
            const((1, LANES)), const((1, LANES)),
            const((1, D_INNER)), const((1, D_INNER)), const((LANES, D_INNER)),
        ],
        out_specs=[pl.BlockSpec((lc, D_INNER), lambda b, c: (b * nc + c, 0)),
                   pl.BlockSpec((1, SSD_HEADS, SSD_HEAD_DIM, D_STATE), lambda b, c: (b, 0, 0, 0))],
        out_shape=[jax.ShapeDtypeStruct((bsz * seq, D_INNER), BF16),
                   jax.ShapeDtypeStruct((bsz, SSD_HEADS, SSD_HEAD_DIM, D_STATE), F32)],
        scratch_shapes=[pltpu.VMEM((lc + 8, CONV_CH), F32), pltpu.VMEM((lc, LANES), F32),
                        pltpu.VMEM((D_STATE, D_INNER), F32)],
        compiler_params=_cparams("parallel", "arbitrary"),
        name="ssd_prompt",
    )(ya, ya, ya, ya, yb, dummy_cs, dummy_h0, k['cw'], k['cb'], k['dtb'], k['aneg'],
      k['dsk'], k['nrm'], k['expand'])


def _ssd_sample(ya3, yb3, conv_state, ssm_state, k):
    bsz, lq, _ = ya3.shape
    lc = SSD_CHUNK
    const = lambda shape: pl.BlockSpec(shape, lambda b, c: (0,) * len(shape))
    return pl.pallas_call(
        functools.partial(_ssd_kernel, rows=lq, has_state=True, lead=True),
        grid=(bsz, 1),
        in_specs=[
            pl.BlockSpec((1, lq, D_INNER), lambda b, c: (b, 0, A_XS // D_INNER)),
            pl.BlockSpec((1, lq, D_INNER), lambda b, c: (b, 0, A_Z // D_INNER)),
            pl.BlockSpec((1, lq, BC_DIM), lambda b, c: (b, 0, A_BM // BC_DIM)),
            pl.BlockSpec((1, lq, BC_DIM), lambda b, c: (b, 0, A_CM // BC_DIM)),
            pl.BlockSpec((1, lq, LANES), lambda b, c: (b, 0, B_SM // LANES)),
            pl.BlockSpec((1, CONV_W - 1, CONV_CH), lambda b, c: (b, 0, 0)),
            pl.BlockSpec((1, SSD_HEADS, SSD_HEAD_DIM, D_STATE), lambda b, c: (b, 0, 0, 0)),
            const((CONV_W, CONV_CH)), const((1, CONV_CH)),
            const((1, LANES)), const((1, LANES)),
            const((1, D_INNER)), const((1, D_INNER)), const((LANES, D_INNER)),
        ],
        out_specs=[pl.BlockSpec((1, lq, D_INNER), lambda b, c: (b, 0, 0)),
                   pl.BlockSpec((1, SSD_HEADS, SSD_HEAD_DIM, D_STATE), lambda b, c: (b, 0, 0, 0))],
        out_shape=[jax.ShapeDtypeStruct((bsz, lq, D_INNER), BF16),
                   jax.ShapeDtypeStruct((bsz, SSD_HEADS, SSD_HEAD_DIM, D_STATE), F32)],
        scratch_shapes=[pltpu.VMEM((lc + 8, CONV_CH), F32), pltpu.VMEM((lc, LANES), F32),
                        pltpu.VMEM((D_STATE, D_INNER), F32)],
        compiler_params=_cparams("parallel", "arbitrary"),
        name="ssd_sample",
    )(ya3, ya3, ya3, ya3, yb3, conv_state, ssm_state, k['cw'], k['cb'], k['dtb'], k['aneg'],
      k['dsk'], k['nrm'], k['expand'])


def _merge_kernel(x_ref, gt_ref, at_ref, sd_ref, ga_ref, gb_ref, wa_ref, wb_ref, wo_ref, o_ref):
    ma = _dot(at_ref[...], wa_ref[...])
    mb = _dot(sd_ref[...], wb_ref[...])
    merged = jax.nn.sigmoid(ga_ref[...]) * ma + jax.nn.sigmoid(gb_ref[...]) * mb
    o_ref[...] = x_ref[...] + gt_ref[0] * _dot(merged.astype(BF16), wo_ref[...])


def _merge(x, gt, attn, ssd, ya, wa, wb, wo, *, tm):
    t, d = x.shape
    groups, r, _ = gt.shape
    tpg = (t // groups) // tm
    return pl.pallas_call(
        _merge_kernel,
        grid=(t // tm,),
        in_specs=[
            pl.BlockSpec((tm, d), lambda i: (i, 0)),
            pl.BlockSpec((1, r, d), lambda i: (i // tpg, 0, 0)),
            pl.BlockSpec((tm, ATT_DIM), lambda i: (i, 0)),
            pl.BlockSpec((tm, D_INNER), lambda i: (i, 0)),
            pl.BlockSpec((tm, d), lambda i: (i, A_GA // D_MODEL)),
            pl.BlockSpec((tm, d), lambda i: (i, A_GB // D_MODEL)),
            pl.BlockSpec((ATT_DIM, d), lambda i: (0, 0)),
            pl.BlockSpec((D_INNER, d), lambda i: (0, 0)),
            pl.BlockSpec((d, d), lambda i: (0, 0)),
        ],
        out_specs=pl.BlockSpec((tm, d), lambda i: (i, 0)),
        out_shape=jax.ShapeDtypeStruct((t, d), F32),
        compiler_params=_cparams("parallel"),
        name="merge",
    )(x, gt, attn, ssd, ya, ya, wa, wb, wo)


def _layer_weights(p, l):
    w_in = p['w_in'][l]
    offs = {}
    o = 0
    for name, size in (('q', ATT_DIM), ('k', KV_DIM), ('v', KV_DIM), ('iq', IDX_HEADS * IDX_DIM),
                       ('ik', IDX_DIM), ('iw', IDX_HEADS), ('z', D_INNER), ('xbc', CONV_CH),
                       ('dt', SSD_HEADS), ('ga', D_MODEL), ('gb', D_MODEL)):
        offs[name] = (o, o + size)
        o += size

    def seg(name):
        s, e = offs[name]
        return w_in[:, s:e]

    xbc = seg('xbc')
    w_a = jnp.concatenate([xbc[:, :D_INNER], seg('z'), seg('ga'), seg('gb'),
                           xbc[:, D_INNER:D_INNER + BC_DIM], xbc[:, D_INNER + BC_DIM:]], axis=1)
    small = jnp.concatenate([seg('ik'), seg('iw'), seg('dt')], axis=1)
    small = jnp.pad(small, ((0, 0), (0, B_DIM - B_SM - small.shape[1])))
    w_b = jnp.concatenate([seg('q'), seg('k'), seg('v'), seg('iq'), small], axis=1)
    return dict(
        w_a=w_a.astype(BF16), w_b=w_b.astype(BF16),
        up1=p['w_ffn1_up'][l].astype(BF16), down1=p['w_ffn1_down'][l].astype(BF16),
        up2=p['w_ffn2_up'][l].astype(BF16), down2=p['w_ffn2_down'][l].astype(BF16),
        wa=p['w_br_a'][l].astype(BF16), wb=p['w_br_b'][l].astype(BF16), wo=p['w_out'][l].astype(BF16),
        g1=p['norm_ffn1'][l].reshape(1, D_MODEL), gm=p['norm_mix'][l].reshape(1, D_MODEL),
        g2=p['norm_ffn2'][l].reshape(1, D_MODEL))


def _pre_mix(x, mods, w, tables, tm):
    sh1, sc1, g1, sh2, sc2 = mods[0], mods[1], mods[2], mods[3], mods[4]
    x = _ffn(x, sh1, sc1, g1, w['g1'], w['g1'], w['up1'], w['down1'], tm=tm, final_norm=False)
    ya = _proj_a(x, sh2, sc2, w['gm'], w['w_a'], tm=tm)
    yb, ybh = _proj_b(x, sh2, sc2, w['gm'], w['w_b'], tables[0], tables[1], tm=tm)
    return x, ya, yb, ybh


def _post_mix(x, mods, w, attn, ssd, ya, gf, tm, tm_merge, final_norm):
    x = _merge(x, mods[5], attn, ssd, ya, w['wa'], w['wb'], w['wo'], tm=tm_merge)
    return _ffn(x, mods[6], mods[7], mods[8], w['g2'], gf, w['up2'], w['down2'],
                tm=tm, final_norm=final_norm)


def kernel(x_prompt, x_sample, cache_k, cache_v, cache_idx_k, state_ssm, state_conv, page_table,
           c_prompt, c_sample, w_ada, b_ada, norm_ffn1, w_ffn1_up, w_ffn1_down, norm_mix, w_in,
           conv_w, conv_b, dt_bias, a_log, d_skip, norm_ssd, w_br_a, w_br_b, w_out,
           norm_ffn2, w_ffn2_up, w_ffn2_down, norm_final):
    p = {'w_ada': w_ada, 'b_ada': b_ada, 'norm_ffn1': norm_ffn1, 'w_ffn1_up': w_ffn1_up,
         'w_ffn1_down': w_ffn1_down, 'norm_mix': norm_mix, 'w_in': w_in, 'conv_w': conv_w,
         'conv_b': conv_b, 'dt_bias': dt_bias, 'a_log': a_log, 'd_skip': d_skip,
         'norm_ssd': norm_ssd, 'w_br_a': w_br_a, 'w_br_b': w_br_b, 'w_out': w_out,
         'norm_ffn2': norm_ffn2, 'w_ffn2_up': w_ffn2_up, 'w_ffn2_down': w_ffn2_down}
    depth = w_in.shape[0]
    bsz, seq, d = x_prompt.shape
    dbsz, dseq, _ = x_sample.shape
    n_pages = page_table.shape[1]
    past_len = n_pages * PAGE_SIZE
    n_pool = cache_k.shape[1]
    tp = bsz * seq
    ts = dbsz * dseq
    tm_p = min(1024, seq)
    tm_merge = min(256, seq)

    rows = bsz + dbsz
    rpad = -rows % 8
    c_all = jnp.pad(jnp.concatenate([c_prompt, c_sample], axis=0), ((0, rpad), (0, 0)))
    mods_all = _ada(c_all, w_ada, b_ada)

    def mods_for(l):
        mp, ms = [], []
        for i in range(N_MODS):
            m = mods_all[l, :, i * d:(i + 1) * d]
            mp.append(m[:bsz].reshape(bsz, 1, d))
            ms.append(jnp.repeat(m[bsz:rows], dseq, axis=0).reshape(1, ts, d))
        return mp, ms

    tab_p = _rope_tables(jnp.arange(seq, dtype=I32))
    tab_s = _rope_tables(past_len + jnp.tile(jnp.arange(dseq, dtype=I32), dbsz))
    gf = norm_final.reshape(1, d)
    n_top_p = min(INDEX_TOPK, seq // 4)
    n_top_s = min(INDEX_TOPK, (past_len + dseq) // 4)

    xp = x_prompt.reshape(tp, d)
    xs = x_sample.reshape(ts, d)
    outs = {k: [] for k in ('kp', 'vp', 'ikp', 'sp', 'cp', 'ks', 'vs', 'iks', 'ss', 'cs')}
    for l in range(depth):
        w = _layer_weights(p, l)
        kc = _ssd_consts(p, l)
        mp, ms = mods_for(l)
        last = l == depth - 1

        xp, ya, yb, ybh = _pre_mix(xp, mp, w, tab_p, tm_p)
        attn = _dsa_prompt(yb, ybh, bsz, seq, n_top_p)
        ssd, h_fin = _ssd_prompt(ya, yb, bsz, seq, kc)
        xp = _post_mix(xp, mp, w, attn, ssd, ya, gf, tm_p, tm_merge, last)
        yb4 = yb.reshape(bsz, seq, B_DIM)
        ya4 = ya.reshape(bsz, seq, A_DIM)
        outs['kp'].append(yb4[:, :, B_K:B_K + KV_DIM].reshape(bsz, seq, KV_HEADS, HEAD_DIM))
        outs['vp'].append(yb4[:, :, B_V:B_V + KV_DIM].reshape(bsz, seq, KV_HEADS, HEAD_DIM))
        outs['ikp'].append(yb4[:, :, B_SM:B_SM + IDX_DIM])
        outs['sp'].append(h_fin)
        tail = ya4[:, seq - (CONV_W - 1):, :]
        outs['cp'].append(jnp.concatenate(
            [tail[:, :, A_XS:A_XS + D_INNER], tail[:, :, A_BM:A_BM + BC_DIM],
             tail[:, :, A_CM:A_CM + BC_DIM]], axis=-1))

        xs, ya, yb, _ = _pre_mix(xs, ms, w, tab_s, ts)
        ya3 = ya.reshape(dbsz, dseq, A_DIM)
        yb3 = yb.reshape(dbsz, dseq, B_DIM)
        rpad_s = PAGE_SIZE - dseq
        sm_pad = jnp.pad(yb3[:, :, B_SM:B_SM + LANES], ((0, 0), (0, rpad_s), (0, 0)))
        kn_pad = jnp.pad(yb3[:, :, B_K:B_K + KV_DIM], ((0, 0), (0, rpad_s), (0, 0)))
        vn_pad = jnp.pad(yb3[:, :, B_V:B_V + KV_DIM], ((0, 0), (0, rpad_s), (0, 0)))
        s_past, s_new = _sample_scores(page_table, yb3, sm_pad, cache_idx_k[l])
        s_fill = jnp.full((dbsz, dseq, SEL_CK - PAGE_SIZE), -jnp.inf, F32)
        scores = jnp.concatenate([s_past, s_new, s_fill], axis=-1).reshape(ts, past_len + SEL_CK)
        bias = _sample_select(scores, n_top_s).reshape(dbsz, dseq, past_len + SEL_CK)
        attn = _sample_attn(page_table, yb3, bias, kn_pad, vn_pad,
                            cache_k[l].reshape(n_pool, PAGE_SIZE, KV_DIM),
                            cache_v[l].reshape(n_pool, PAGE_SIZE, KV_DIM))
        ssd, h_fin = _ssd_sample(ya3, yb3, state_conv[l], state_ssm[l], kc)
        xs = _post_mix(xs, ms, w, attn.reshape(ts, ATT_DIM), ssd.reshape(ts, D_INNER), ya, gf,
                       ts, ts, last)
        outs['ks'].append(yb3[:, :, B_K:B_K + KV_DIM].reshape(dbsz, dseq, KV_HEADS, HEAD_DIM))
        outs['vs'].append(yb3[:, :, B_V:B_V + KV_DIM].reshape(dbsz, dseq, KV_HEADS, HEAD_DIM))
        outs['iks'].append(yb3[:, :, B_SM:B_SM + IDX_DIM])
        outs['ss'].append(h_fin)
        xbc = jnp.concatenate([ya3[:, :, A_XS:A_XS + D_INNER], ya3[:, :, A_BM:A_BM + BC_DIM],
                               ya3[:, :, A_CM:A_CM + BC_DIM]], axis=-1)
        xpad = jnp.concatenate([state_conv[l], xbc], axis=1)
        outs['cs'].append(xpad[:, dseq:])

    return (xp.reshape(bsz, seq, d), xs.reshape(dbsz, dseq, d),
            jnp.stack(outs['kp']), jnp.stack(outs['vp']), jnp.stack(outs['ikp']),
            jnp.stack(outs['sp']), jnp.stack(outs['cp']),
            jnp.stack(outs['ks']), jnp.stack(outs['vs']), jnp.stack(outs['iks']),
            jnp.stack(outs['ss']), jnp.stack(outs['cs']))
```

```python
import functools
import math

import jax
import jax.numpy as jnp
from jax import lax
from jax.experimental import pallas as pl
from jax.experimental.pallas import tpu as pltpu

F32 = jnp.float32
BF16 = jnp.bfloat16
I32 = jnp.int32

D_MODEL = 1024
PAGE_SIZE = 128
N_HEADS = 8
HEAD_DIM = 128
KV_HEADS = 4
Q_PER_KV = N_HEADS // KV_HEADS
ATT_DIM = N_HEADS * HEAD_DIM
KV_DIM = KV_HEADS * HEAD_DIM
ATT_SCALE = HEAD_DIM ** -0.5
IDX_HEADS = 8
IDX_DIM = 64
IDX_ROPE_DIM = 32
IDX_SCALE = IDX_HEADS ** -0.5 * IDX_DIM ** -0.5
INDEX_TOPK = 256
ROPE_THETA = 10000.0
D_INNER = 2 * D_MODEL
SSD_HEAD_DIM = 64
SSD_HEADS = D_INNER // SSD_HEAD_DIM
SSD_GROUPS = 4
D_STATE = 128
CONV_W = 4
BC_DIM = SSD_GROUPS * D_STATE
CONV_CH = D_INNER + 2 * BC_DIM
SSD_CHUNK = 128
FFN_DIM = 2816
N_MODS = 9
EPS = 1e-6

LANES = 128
NEG_BIG = -1e30
INT_MIN = -(2 ** 31)
NEG_INF_KEY = -2139095041
VMEM_LIMIT = 56 * 1024 * 1024

A_XS, A_Z, A_GA, A_GB, A_BM, A_CM = 0, 2048, 4096, 5120, 6144, 6656
A_DIM = 7168
B_Q, B_K, B_V, B_IQ, B_SM = 0, 1024, 1536, 2048, 2560
B_DIM = 3072
SM_IW = IDX_DIM
SM_DT = IDX_DIM + IDX_HEADS
PROJ_TN = 512


def _cparams(*sem):
    return pltpu.CompilerParams(dimension_semantics=sem, vmem_limit_bytes=VMEM_LIMIT)


def _silu(v):
    return v * jax.nn.sigmoid(v)


def _norm_mod(x, g, shift, scale):
    xn = x * lax.rsqrt(jnp.mean(x * x, axis=-1, keepdims=True) + EPS) * g
    return xn * (1.0 + scale) + shift


def _dot(a, b):
    return jnp.dot(a, b, preferred_element_type=F32)


def _dot_nt(a, b):
    return lax.dot_general(a, b, (((1,), (1,)), ((), ())), preferred_element_type=F32)


def _ada_kernel(c_ref, w_ref, b_ref, o_ref):
    s = _silu(c_ref[...]).astype(BF16)
    o_ref[0] = _dot(s, w_ref[0].astype(BF16)) + b_ref[0]


def _ada(c_all, w_ada, b_ada):
    depth, d, n = w_ada.shape
    r = c_all.shape[0]
    tn = 1024
    return pl.pallas_call(
        _ada_kernel,
        grid=(depth, n // tn),
        in_specs=[
            pl.BlockSpec((r, d), lambda l, j: (0, 0)),
            pl.BlockSpec((1, d, tn), lambda l, j: (l, 0, j)),
            pl.BlockSpec((1, 1, tn), lambda l, j: (l, 0, j)),
        ],
        out_specs=pl.BlockSpec((1, r, tn), lambda l, j: (l, 0, j)),
        out_shape=jax.ShapeDtypeStruct((depth, r, n), F32),
        compiler_params=_cparams("parallel", "parallel"),
        name="adaln",
    )(c_all, w_ada, b_ada.reshape(depth, 1, n))


def _ffn_kernel(x_ref, sh_ref, sc_ref, gt_ref, g_ref, gf_ref, wg_ref, wu_ref, wd_ref,
                o_ref, h_sc, acc_sc, *, final_norm):
    j = pl.program_id(1)

    @pl.when(j == 0)
    def _():
        h_sc[...] = _norm_mod(x_ref[...], g_ref[...], sh_ref[0], sc_ref[0]).astype(BF16)
        acc_sc[...] = jnp.zeros_like(acc_sc)

    h = h_sc[...]
    a = _dot(h, wg_ref[...])
    b = _dot(h, wu_ref[...])
    t = (_silu(a) * b).astype(BF16)
    acc_sc[...] += _dot(t, wd_ref[...])

    @pl.when(j == pl.num_programs(1) - 1)
    def _():
        y = x_ref[...] + 0.5 * gt_ref[0] * acc_sc[...]
        if final_norm:
            y = y * lax.rsqrt(jnp.mean(y * y, axis=-1, keepdims=True) + EPS) * gf_ref[...]
        o_ref[...] = y


def _ffn(x, sh, sc, gt, g, gf, w_up, w_down, *, tm, final_norm):
    t, d = x.shape
    f = w_down.shape[0]
    tf = 256
    nf = f // tf
    groups, r, _ = sh.shape
    tpg = (t // groups) // tm
    mod_spec = pl.BlockSpec((1, r, d), lambda i, j: (i // tpg, 0, 0))
    return pl.pallas_call(
        functools.partial(_ffn_kernel, final_norm=final_norm),
        grid=(t // tm, nf),
        in_specs=[
            pl.BlockSpec((tm, d), lambda i, j: (i, 0)),
            mod_spec, mod_spec, mod_spec,
            pl.BlockSpec((1, d), lambda i, j: (0, 0)),
            pl.BlockSpec((1, d), lambda i, j: (0, 0)),
            pl.BlockSpec((d, tf), lambda i, j: (0, j)),
            pl.BlockSpec((d, tf), lambda i, j: (0, j + nf)),
            pl.BlockSpec((tf, d), lambda i, j: (j, 0)),
        ],
        out_specs=pl.BlockSpec((tm, d), lambda i, j: (i, 0)),
        out_shape=jax.ShapeDtypeStruct((t, d), F32),
        scratch_shapes=[pltpu.VMEM((tm, d), BF16), pltpu.VMEM((tm, d), F32)],
        compiler_params=_cparams("parallel", "arbitrary"),
        name="ffn",
    )(x, sh, sc, gt, g, gf, w_up, w_up, w_down)


def _proj_a_kernel(x_ref, sh_ref, sc_ref, g_ref, w_ref, o_ref, h_sc):
    @pl.when(pl.program_id(1) == 0)
    def _():
        h_sc[...] = _norm_mod(x_ref[...], g_ref[...], sh_ref[0], sc_ref[0]).astype(BF16)

    o_ref[...] = _dot(h_sc[...], w_ref[...])


def _proj_b_kernel(x_ref, sh_ref, sc_ref, g_ref, w_ref, t64_ref, t16_ref, o_ref, ob_ref, h_sc):
    j = pl.program_id(1)

    @pl.when(j == 0)
    def _():
        h_sc[...] = _norm_mod(x_ref[...], g_ref[...], sh_ref[0], sc_ref[0]).astype(BF16)

    y = _dot(h_sc[...], w_ref[...])
    groups = y.shape[1] // LANES

    def emit(g, v):
        o_ref[:, g * LANES:(g + 1) * LANES] = v
        ob_ref[:, g * LANES:(g + 1) * LANES] = v.astype(BF16)

    @pl.when(j < B_V // PROJ_TN)
    def _():
        c = t64_ref[0, :, 0:LANES]
        a = t64_ref[0, :, LANES:2 * LANES]
        for g in range(groups):
            yg = y[:, g * LANES:(g + 1) * LANES]
            emit(g, yg * c + pltpu.roll(yg, HEAD_DIM // 2, 1) * a)

    @pl.when(j == B_V // PROJ_TN)
    def _():
        for g in range(groups):
            emit(g, y[:, g * LANES:(g + 1) * LANES])

    @pl.when(j > B_V // PROJ_TN)
    def _():
        half = IDX_ROPE_DIM // 2
        c = t16_ref[0, :, 0:LANES]
        a = t16_ref[0, :, LANES:2 * LANES]
        b = t16_ref[0, :, 2 * LANES:3 * LANES]
        for g in range(groups):
            yg = y[:, g * LANES:(g + 1) * LANES]
            emit(g, yg * c + pltpu.roll(yg, half, 1) * a + pltpu.roll(yg, LANES - half, 1) * b)


def _proj_a(x, sh, sc, g, w, *, tm):
    t, d = x.shape
    n = w.shape[1]
    groups, r, _ = sh.shape
    tpg = (t // groups) // tm
    mod_spec = pl.BlockSpec((1, r, d), lambda i, j: (i // tpg, 0, 0))
    return pl.pallas_call(
        _proj_a_kernel,
        grid=(t // tm, n // PROJ_TN),
        in_specs=[
            pl.BlockSpec((tm, d), lambda i, j: (i, 0)),
            mod_spec, mod_spec,
            pl.BlockSpec((1, d), lambda i, j: (0, 0)),
            pl.BlockSpec((d, PROJ_TN), lambda i, j: (0, j)),
        ],
        out_specs=pl.BlockSpec((tm, PROJ_TN), lambda i, j: (i, j)),
        out_shape=jax.ShapeDtypeStruct((t, n), F32),
        scratch_shapes=[pltpu.VMEM((tm, d), BF16)],
        compiler_params=_cparams("parallel", "arbitrary"),
        name="proj_a",
    )(x, sh, sc, g, w)


def _proj_b(x, sh, sc, g, w, t64, t16, *, tm):
    t, d = x.shape
    n = w.shape[1]
    groups, r, _ = sh.shape
    tpg = (t // groups) // tm
    npos = t64.shape[1] // tm
    k_tile = B_K // PROJ_TN
    mod_spec = pl.BlockSpec((1, r, d), lambda i, j: (i // tpg, 0, 0))
    sm_tile = B_SM // PROJ_TN
    return pl.pallas_call(
        _proj_b_kernel,
        grid=(t // tm, n // PROJ_TN),
        in_specs=[
            pl.BlockSpec((tm, d), lambda i, j: (i, 0)),
            mod_spec, mod_spec,
            pl.BlockSpec((1, d), lambda i, j: (0, 0)),
            pl.BlockSpec((d, PROJ_TN), lambda i, j: (0, j)),
            pl.BlockSpec((1, tm, 2 * LANES),
                         lambda i, j: (jnp.where(j == k_tile, 1, 0), i % npos, 0)),
            pl.BlockSpec((1, tm, 3 * LANES),
                         lambda i, j: (jnp.where(j == sm_tile, 1, 0), i % npos, 0)),
        ],
        out_specs=[pl.BlockSpec((tm, PROJ_TN), lambda i, j: (i, j)),
                   pl.BlockSpec((tm, PROJ_TN), lambda i, j: (i, j))],
        out_shape=[jax.ShapeDtypeStruct((t, n), F32), jax.ShapeDtypeStruct((t, n), BF16)],
        scratch_shapes=[pltpu.VMEM((tm, d), BF16)],
        compiler_params=_cparams("parallel", "arbitrary"),
        name="proj_b",
    )(x, sh, sc, g, w, t64, t16)


def _rope_tables(pos):
    posf = pos.astype(F32)[:, None]
    half = HEAD_DIM // 2
    inv = ROPE_THETA ** (-jnp.arange(half, dtype=F32) / half)
    ang = posf * inv[None, :]
    cos, sin = jnp.cos(ang), jnp.sin(ang)
    t_k = jnp.concatenate([cos, cos, -sin, sin], axis=1)
    t64 = jnp.stack([t_k * ATT_SCALE, t_k])
    ih = IDX_ROPE_DIM // 2
    inv_i = ROPE_THETA ** (-jnp.arange(ih, dtype=F32) / ih)
    ang_i = posf * inv_i[None, :]
    ci, si = jnp.cos(ang_i), jnp.sin(ang_i)
    n = pos.shape[0]
    ones = jnp.ones((n, IDX_DIM - IDX_ROPE_DIM), F32)
    zeros = jnp.zeros((n, IDX_DIM - IDX_ROPE_DIM), F32)
    zh = jnp.zeros((n, ih), F32)
    c_head = jnp.concatenate([ci, ci, ones], axis=1)
    a_head = jnp.concatenate([zh, si, zeros], axis=1)
    b_head = jnp.concatenate([-si, zh, zeros], axis=1)
    t_iq = jnp.concatenate([c_head, c_head, a_head, a_head, b_head, b_head], axis=1)
    z64 = jnp.zeros((n, IDX_DIM), F32)
    c_tail = jnp.concatenate([jnp.full((n, IDX_HEADS), IDX_SCALE, F32),
                              jnp.ones((n, LANES - IDX_DIM - IDX_HEADS), F32)], axis=1)
    t_sm = jnp.concatenate([c_head, c_tail, a_head, z64, b_head, z64], axis=1)
    return t64, jnp.stack([t_iq, t_sm])


def _sortable(x):
    b = lax.bitcast_convert_type(x, I32)
    b = jnp.where(b == INT_MIN, 0, b)
    return jnp.where(b < 0, b ^ 0x7FFFFFFF, b)


def _select_topk(key_sc, bias_sc, nk, n_top, idx_bits):
    _, rows, ck = key_sc.shape
    nblk = ck // LANES
    rb = min(rows, 128)

    def col_index(nrows, c, t):
        return lax.broadcasted_iota(I32, (nrows, LANES), 1) + (c * ck + t * LANES)

    def lanes(rep, rs):
        return rep[rs]

    def row_sum(acc):
        return jnp.broadcast_to(jnp.sum(acc, axis=1, keepdims=True), acc.shape)

    def count(make_pred):
        slices = [slice(r * rb, (r + 1) * rb) for r in range(rows // rb)]
        preds = [make_pred(rs) for rs in slices]
        accs = []
        for rs, pred in zip(slices, preds):
            def body(c, acc, rs=rs, pred=pred):
                for t in range(nblk):
                    k = key_sc[c, rs, t * LANES:(t + 1) * LANES]
                    acc = acc + jnp.where(pred(k, c, t), 1.0, 0.0)
                return acc
            accs.append(lax.fori_loop(0, nk, body, jnp.zeros((rb, LANES), F32)))
        acc = accs[0] if len(accs) == 1 else jnp.concatenate(accs, axis=0)
        return row_sum(acc)

    def ge(col):
        def make(rs):
            cb = lanes(col, rs)
            return lambda k, c, t: k >= cb
        return make

    def gt(col):
        def make(rs):
            cb = lanes(col, rs)
            return lambda k, c, t: k > cb
        return make

    k_top = float(n_top)

    def vstep(it, lo):
        cand = lo + lax.shift_left(jnp.int32(1), 31 - it)
        cnt = count(ge(cand))
        return jnp.where(cnt >= k_top, cand, lo)

    lo = lax.fori_loop(0, 32, vstep, jnp.full((rows, LANES), INT_MIN, I32))
    cnt_gt = count(gt(lo))
    cnt_ge = count(ge(lo))
    need = k_top - cnt_gt
    excess = (cnt_ge > k_top) & (lo > NEG_INF_KEY)
    any_excess = jnp.max(excess.astype(I32)) > 0

    @pl.when(jnp.logical_not(any_excess))
    def _():
        thr = jnp.maximum(lo, NEG_INF_KEY + 1)

        def wbody(c, carry):
            for t in range(nblk):
                sl = slice(t * LANES, (t + 1) * LANES)
                bias_sc[c, :, sl] = jnp.where(key_sc[c, :, sl] >= thr, 0.0, NEG_BIG)
            return carry

        lax.fori_loop(0, nk, wbody, 0)

    @pl.when(any_excess)
    def _():
        def tie_below(cand):
            def make(rs):
                lob, cb = lanes(lo, rs), lanes(cand, rs)
                return lambda k, c, t: (k == lob) & (col_index(rb, c, t) < cb)
            return make

        def jstep(it, m):
            cand = m + lax.shift_left(jnp.int32(1), idx_bits - 1 - it)
            f = count(tie_below(cand))
            return jnp.where(f < need, cand, m)
        m = lax.fori_loop(0, idx_bits, jstep, jnp.zeros((rows, LANES), I32))
        jmax = jnp.where(excess, m, jnp.int32(2 ** 30))

        def wbody(c, carry):
            for t in range(nblk):
                sl = slice(t * LANES, (t + 1) * LANES)
                k = key_sc[c, :, sl]
                sel = (k > lo) | ((k == lo) & (col_index(rows, c, t) <= jmax))
                sel = sel & (k > NEG_INF_KEY)
                bias_sc[c, :, sl] = jnp.where(sel, 0.0, NEG_BIG)
            return carry

        lax.fori_loop(0, nk, wbody, 0)


def _flash_step(s, v, m_prev, l_prev, acc_prev):
    nblk = s.shape[1] // LANES
    m_new = jnp.maximum(m_prev, jnp.max(s, axis=1, keepdims=True))
    alpha = jnp.exp(m_prev - m_new)
    p = [jnp.exp(s[:, t * LANES:(t + 1) * LANES] - m_new) for t in range(nblk)]
    psum = p[0]
    for t in range(1, nblk):
        psum = psum + p[t]
    l_new = alpha * l_prev + jnp.sum(psum, axis=1, keepdims=True)
    pb = jnp.concatenate([pt.astype(BF16) for pt in p], axis=1)
    acc_new = alpha * acc_prev + _dot(pb, v)
    return m_new, l_new, acc_new


ATT_CK = 512


def _dsa_prompt_kernel(q_ref, iq_ref, smq_ref, k_ref, v_ref, smk_ref, o_ref,
                       key_sc, bias_sc, m_sc, l_sc, acc_sc, *, n_top, idx_bits):
    qi = pl.program_id(1)
    tq, ck = q_ref.shape[0], ATT_CK
    nk = (qi * tq + tq + ck - 1) // ck
    lane = lax.broadcasted_iota(I32, (tq, LANES), 1)

    iq = iq_ref[...].astype(F32)
    iqh = []
    for p in range(IDX_HEADS // 2):
        grp = iq[:, p * LANES:(p + 1) * LANES]
        iqh.append(jnp.where(lane < IDX_DIM, grp, 0.0).astype(BF16))
        iqh.append(jnp.where(lane < IDX_DIM, pltpu.roll(grp, IDX_DIM, 1), 0.0).astype(BF16))
    smq = smq_ref[...]
    qpos = qi * tq + lax.broadcasted_iota(I32, (tq, ck), 0)

    def score_body(c, carry):
        kc = smk_ref[pl.ds(pl.multiple_of(c * ck, ck), ck), :]
        sc = jnp.zeros((tq, ck), F32)
        for h in range(IDX_HEADS):
            lg = _dot_nt(iqh[h], kc)
            sc = sc + jnp.maximum(lg, 0.0) * smq[:, SM_IW + h:SM_IW + h + 1]
        kpos = c * ck + lax.broadcasted_iota(I32, (tq, ck), 1)
        sc = jnp.where(kpos <= qpos, sc, -jnp.inf)
        key_sc[c] = _sortable(sc)
        return carry

    lax.fori_loop(0, nk, score_body, 0)
    _select_topk(key_sc, bias_sc, nk, n_top, idx_bits)

    m_sc[...] = jnp.full(m_sc.shape, NEG_BIG, F32)
    l_sc[...] = jnp.zeros_like(l_sc)
    acc_sc[...] = jnp.zeros_like(acc_sc)

    def att_body(c, carry):
        off = pl.multiple_of(c * ck, ck)
        bias = bias_sc[c]
        bias2 = jnp.concatenate([bias] * Q_PER_KV, axis=0)
        new = []
        for g in range(KV_HEADS):
            q2 = jnp.concatenate(
                [q_ref[:, (Q_PER_KV * g + t) * HEAD_DIM:(Q_PER_KV * g + t + 1) * HEAD_DIM]
                 for t in range(Q_PER_KV)], axis=0)
            kc = k_ref[pl.ds(off, ck), g * HEAD_DIM:(g + 1) * HEAD_DIM]
            vc = v_ref[pl.ds(off, ck), g * HEAD_DIM:(g + 1) * HEAD_DIM]
            s = _dot_nt(q2, kc) + bias2
            new.append(_flash_step(s, vc, m_sc[g], l_sc[g], acc_sc[g]))
        for g, (m_new, l_new, acc_new) in enumerate(new):
            m_sc[g] = m_new
            l_sc[g] = l_new
            acc_sc[g] = acc_new
        return carry

    lax.fori_loop(0, nk, att_body, 0)
    for g in range(KV_HEADS):
        o = acc_sc[g] / l_sc[g]
        for t in range(Q_PER_KV):
            h = Q_PER_KV * g + t
            o_ref[:, h * HEAD_DIM:(h + 1) * HEAD_DIM] = o[t * tq:(t + 1) * tq].astype(o_ref.dtype)


def _dsa_prompt(yb, ybh, bsz, seq, n_top):
    tq = min(256, seq)
    nq = seq // tq
    nch = seq // ATT_CK
    kernel = functools.partial(_dsa_prompt_kernel, n_top=n_top,
                               idx_bits=max(1, (seq - 1).bit_length()))
    return pl.pallas_call(
        kernel,
        grid=(bsz, nq),
        in_specs=[
            pl.BlockSpec((tq, ATT_DIM), lambda b, i: (b * nq + i, B_Q // ATT_DIM)),
            pl.BlockSpec((tq, 512), lambda b, i: (b * nq + i, B_IQ // 512)),
            pl.BlockSpec((tq, LANES), lambda b, i: (b * nq + i, B_SM // LANES)),
            pl.BlockSpec((seq, KV_DIM), lambda b, i: (b, B_K // KV_DIM)),
            pl.BlockSpec((seq, KV_DIM), lambda b, i: (b, B_V // KV_DIM)),
            pl.BlockSpec((seq, LANES), lambda b, i: (b, B_SM // LANES)),
        ],
        out_specs=pl.BlockSpec((tq, ATT_DIM), lambda b, i: (b * nq + i, 0)),
        out_shape=jax.ShapeDtypeStruct((bsz * seq, ATT_DIM), BF16),
        scratch_shapes=[pltpu.VMEM((nch, tq, ATT_CK), I32),
                        pltpu.VMEM((nch, tq, ATT_CK), F32),
                        pltpu.VMEM((KV_HEADS, Q_PER_KV * tq, LANES), F32),
                        pltpu.VMEM((KV_HEADS, Q_PER_KV * tq, LANES), F32),
                        pltpu.VMEM((KV_HEADS, Q_PER_KV * tq, HEAD_DIM), F32)],
        compiler_params=_cparams("parallel", "arbitrary"),
        name="dsa_prompt",
    )(ybh, ybh, yb, ybh, ybh, ybh)


PAGES_PER_STEP = 8


def _sample_score_kernel(pt_ref, iq_ref, sm_ref, smn_ref, *refs):
    del pt_ref
    npg = PAGES_PER_STEP
    ik_refs = refs[:npg]
    sp_ref, sn_ref = refs[npg], refs[npg + 1]
    lq = iq_ref.shape[1]
    iq = iq_ref[0]
    sm = sm_ref[0]
    iqh = jnp.concatenate([iq[:, h * IDX_DIM:(h + 1) * IDX_DIM] for h in range(IDX_HEADS)],
                          axis=0).astype(BF16)

    def score(lg):
        sc = jnp.zeros((lq, lg.shape[1]), F32)
        for h in range(IDX_HEADS):
            sc = sc + jnp.maximum(lg[h * lq:(h + 1) * lq], 0.0) * sm[:, SM_IW + h:SM_IW + h + 1]
        return sc

    for i in range(npg):
        sp_ref[0, :, i * PAGE_SIZE:(i + 1) * PAGE_SIZE] = score(
            _dot(iqh, ik_refs[i][0, 0].astype(BF16)))

    @pl.when(pl.program_id(1) == 0)
    def _():
        sc = score(_dot_nt(iqh, smn_ref[0][:, 0:IDX_DIM].astype(BF16)))
        col = lax.broadcasted_iota(I32, sc.shape, 1)
        row = lax.broadcasted_iota(I32, sc.shape, 0)
        sn_ref[0] = jnp.where(col <= row, sc, -jnp.inf)


def _sample_scores(page_table, yb3, sm_pad, pool_ik_t, layer):
    bsz, lq, _ = yb3.shape
    n_pages = page_table.shape[1]
    npg = PAGES_PER_STEP
    steps = n_pages // npg
    ik_specs = [pl.BlockSpec((1, 1, IDX_DIM, PAGE_SIZE),
                             functools.partial(
                                 lambda b, j, pt, i: (layer, pt[b, j * npg + i], 0, 0), i=i))
                for i in range(npg)]
    grid_spec = pltpu.PrefetchScalarGridSpec(
        num_scalar_prefetch=1,
        grid=(bsz, steps),
        in_specs=[
            pl.BlockSpec((1, lq, 512), lambda b, j, pt: (b, 0, B_IQ // 512)),
            pl.BlockSpec((1, lq, LANES), lambda b, j, pt: (b, 0, B_SM // LANES)),
            pl.BlockSpec((1, PAGE_SIZE, LANES), lambda b, j, pt: (b, 0, 0)),
        ] + ik_specs,
        out_specs=[pl.BlockSpec((1, lq, npg * PAGE_SIZE), lambda b, j, pt: (b, 0, j)),
                   pl.BlockSpec((1, lq, PAGE_SIZE), lambda b, j, pt: (b, 0, 0))],
    )
    return pl.pallas_call(
        _sample_score_kernel,
        grid_spec=grid_spec,
        out_shape=[jax.ShapeDtypeStruct((bsz, lq, n_pages * PAGE_SIZE), F32),
                   jax.ShapeDtypeStruct((bsz, lq, PAGE_SIZE), F32)],
        compiler_params=_cparams("parallel", "arbitrary"),
        name="sample_scores",
    )(page_table, yb3, yb3, sm_pad, *([pool_ik_t] * npg))


SEL_CK = 512


def _sample_select_kernel(s_ref, o_ref, key_sc, bias_sc, *, n_top, idx_bits):
    nch = key_sc.shape[0]
    for c in range(nch):
        key_sc[c] = _sortable(s_ref[:, c * SEL_CK:(c + 1) * SEL_CK])
    _select_topk(key_sc, bias_sc, nch, n_top, idx_bits)
    for c in range(nch):
        o_ref[:, c * SEL_CK:(c + 1) * SEL_CK] = bias_sc[c]


def _sample_select(scores, n_top):
    rows, width = scores.shape
    nch = width // SEL_CK
    kernel = functools.partial(_sample_select_kernel, n_top=n_top,
                               idx_bits=max(1, (width - 1).bit_length()))
    return pl.pallas_call(
        kernel,
        grid=(1,),
        in_specs=[pl.BlockSpec((rows, width), lambda i: (0, 0))],
        out_specs=pl.BlockSpec((rows, width), lambda i: (0, 0)),
        out_shape=jax.ShapeDtypeStruct((rows, width), F32),
        scratch_shapes=[pltpu.VMEM((nch, rows, SEL_CK), I32),
                        pltpu.VMEM((nch, rows, SEL_CK), F32)],
        compiler_params=_cparams("arbitrary"),
        name="sample_select",
    )(scores)


def _sample_attn_kernel(pt_ref, q_ref, bias_ref, biasn_ref, kn_ref, vn_ref, *refs,):
    del pt_ref
    npg = PAGES_PER_STEP
    k_refs = refs[:npg]
    v_refs = refs[npg:2 * npg]
    o_ref, m_sc, l_sc, acc_sc = refs[2 * npg:]
    j = pl.program_id(1)
    lq = q_ref.shape[1]
    rows = Q_PER_KV * lq

    @pl.when(j == 0)
    def _():
        m_sc[...] = jnp.full(m_sc.shape, NEG_BIG, F32)
        l_sc[...] = jnp.zeros_like(l_sc)
        acc_sc[...] = jnp.zeros_like(acc_sc)

    q = q_ref[0]

    def q_group(g):
        return jnp.concatenate(
            [q[:, (Q_PER_KV * g + t) * HEAD_DIM:(Q_PER_KV * g + t + 1) * HEAD_DIM]
             for t in range(Q_PER_KV)], axis=0).astype(BF16)

    def accumulate(g, keys, vals, bias):
        s = _dot_nt(q_group(g), keys) + jnp.concatenate([bias] * Q_PER_KV, axis=0)
        m_new, l_new, acc_new = _flash_step(s, vals, m_sc[g], l_sc[g], acc_sc[g])
        m_sc[g] = m_new
        l_sc[g] = l_new
        acc_sc[g] = acc_new

    def head_rows(ref, g):
        return ref[0, 0, pl.ds(g, PAGE_SIZE, stride=KV_HEADS), :]

    bias = bias_ref[0]
    for g in range(KV_HEADS):
        keys = jnp.concatenate([head_rows(k_refs[i], g) for i in range(npg)], axis=0).astype(BF16)
        vals = jnp.concatenate([head_rows(v_refs[i], g) for i in range(npg)], axis=0).astype(BF16)
        accumulate(g, keys, vals, bias)

    @pl.when(j == pl.num_programs(1) - 1)
    def _():
        bias_n = biasn_ref[0]
        for g in range(KV_HEADS):
            accumulate(g, head_rows(kn_ref, g).astype(BF16), head_rows(vn_ref, g).astype(BF16),
                       bias_n)
            o = acc_sc[g] / l_sc[g]
            for t in range(Q_PER_KV):
                h = Q_PER_KV * g + t
                o_ref[0, :, h * HEAD_DIM:(h + 1) * HEAD_DIM] = o[t * lq:(t + 1) * lq].astype(o_ref.dtype)


def _sample_attn(page_table, yb3, bias3, kn_pad, vn_pad, pool_k, pool_v, layer):
    bsz, lq, _ = yb3.shape
    n_pages = page_table.shape[1]
    npg = PAGES_PER_STEP
    steps = n_pages // npg
    past = n_pages * PAGE_SIZE
    page_rows = PAGE_SIZE * KV_HEADS

    def page_spec(i):
        return pl.BlockSpec((1, 1, page_rows, HEAD_DIM),
                            functools.partial(
                                lambda b, j, pt, i: (layer, pt[b, j * npg + i], 0, 0), i=i))

    grid_spec = pltpu.PrefetchScalarGridSpec(
        num_scalar_prefetch=1,
        grid=(bsz, steps),
        in_specs=[
            pl.BlockSpec((1, lq, ATT_DIM), lambda b, j, pt: (b, 0, B_Q // ATT_DIM)),
            pl.BlockSpec((1, lq, npg * PAGE_SIZE), lambda b, j, pt: (b, 0, j)),
            pl.BlockSpec((1, lq, PAGE_SIZE), lambda b, j, pt: (b, 0, past // PAGE_SIZE)),
            pl.BlockSpec((1, 1, page_rows, HEAD_DIM), lambda b, j, pt: (b, 0, 0, 0)),
            pl.BlockSpec((1, 1, page_rows, HEAD_DIM), lambda b, j, pt: (b, 0, 0, 0)),
        ] + [page_spec(i) for i in range(npg)] + [page_spec(i) for i in range(npg)],
        out_specs=pl.BlockSpec((1, lq, ATT_DIM), lambda b, j, pt: (b, 0, 0)),
        scratch_shapes=[pltpu.VMEM((KV_HEADS, Q_PER_KV * lq, LANES), F32),
                        pltpu.VMEM((KV_HEADS, Q_PER_KV * lq, LANES), F32),
                        pltpu.VMEM((KV_HEADS, Q_PER_KV * lq, HEAD_DIM), F32)],
    )
    return pl.pallas_call(
        _sample_attn_kernel,
        grid_spec=grid_spec,
        out_shape=jax.ShapeDtypeStruct((bsz, lq, ATT_DIM), BF16),
        compiler_params=_cparams("parallel", "arbitrary"),
        name="sample_attn",
    )(page_table, yb3, bias3, bias3, kn_pad, vn_pad, *([pool_k] * npg), *([pool_v] * npg))


def _ssd_kernel(xs_ref, z_ref, bm_ref, cm_ref, sm_ref, cs_ref, h0_ref, cw_ref, cb_ref,
                dtb_ref, aneg_ref, dsk_ref, nrm_ref, exp_ref,
                y_ref, hout_ref, xp_sc, sm_sc, ht_sc, *, rows, has_state, lead):
    c = pl.program_id(1)
    lc = SSD_CHUNK
    pad = 8
    hist = CONV_W - 1

    def blk(ref):
        return ref[0] if lead else ref[...]

    @pl.when(c == 0)
    def _():
        xp_sc[...] = jnp.zeros_like(xp_sc)
        sm_sc[...] = jnp.zeros_like(sm_sc)
        if has_state:
            xp_sc[pad - hist:pad, :] = cs_ref[0, 0]
            for jp in range(SSD_HEADS // 2):
                pair = jnp.concatenate([h0_ref[0, 0, 2 * jp], h0_ref[0, 0, 2 * jp + 1]], axis=0)
                ht_sc[:, jp * LANES:(jp + 1) * LANES] = pair.T
        else:
            ht_sc[...] = jnp.zeros_like(ht_sc)

    @pl.when(c > 0)
    def _():
        xp_sc[0:pad, :] = xp_sc[lc:lc + pad, :]

    xp_sc[pad:pad + rows, 0:D_INNER] = blk(xs_ref)
    xp_sc[pad:pad + rows, D_INNER:D_INNER + BC_DIM] = blk(bm_ref)
    xp_sc[pad:pad + rows, D_INNER + BC_DIM:CONV_CH] = blk(cm_ref)
    sm_sc[0:rows, :] = blk(sm_ref)

    conv = cb_ref[...]
    for i in range(CONV_W):
        conv = conv + cw_ref[i:i + 1, :] * xp_sc[pad - hist + i:pad - hist + i + lc, :]
    conv = _silu(conv)
    xs = conv[:, 0:D_INNER]
    bmat = conv[:, D_INNER:D_INNER + BC_DIM]
    cmat = conv[:, D_INNER + BC_DIM:CONV_CH]

    lane = lax.broadcasted_iota(I32, (lc, LANES), 1)
    row = lax.broadcasted_iota(I32, (lc, LANES), 0)
    dt_lanes = (lane >= SM_DT) & (lane < SM_DT + SSD_HEADS) & (row < rows)
    raw = sm_sc[...] + dtb_ref[...]
    softplus = jnp.maximum(raw, 0.0) + jnp.log(1.0 + jnp.exp(-jnp.abs(raw)))
    dt = jnp.where(dt_lanes, softplus, 0.0)
    a = dt * aneg_ref[...]
    ltri = (row >= lane).astype(F32)
    acum = jnp.dot(ltri, a, preferred_element_type=F32, precision=lax.Precision.HIGHEST)
    acum_t = acum.T
    dt_t = dt.T
    a_last = acum[lc - 1:lc, :]
    w_end = jnp.exp(a_last - acum) * dt
    expand = exp_ref[...]
    w_exp = _dot(w_end.astype(BF16), expand)
    e_exp = _dot(jnp.exp(acum).astype(BF16), expand)
    cd_exp = jnp.dot(jnp.broadcast_to(jnp.exp(a_last), (8, LANES)), expand.astype(F32),
                     preferred_element_type=F32, precision=lax.Precision.HIGHEST)[0:1, :]
    xw = (xs * w_exp).astype(BF16)
    xs_b = xs.astype(BF16)
    causal = row >= lane

    ht_prev = ht_sc[...]
    gw = D_INNER // SSD_GROUPS
    heads_pg = SSD_HEADS // SSD_GROUPS
    y_parts = []
    for g in range(SSD_GROUPS):
        bg = bmat[:, g * D_STATE:(g + 1) * D_STATE]
        cg = cmat[:, g * D_STATE:(g + 1) * D_STATE].astype(BF16)
        cb = _dot_nt(cg, bg.astype(BF16))
        y_off = _dot(cg, ht_prev[:, g * gw:(g + 1) * gw].astype(BF16)) * e_exp[:, g * gw:(g + 1) * gw]
        states_t = _dot(bg.T.astype(BF16), xw[:, g * gw:(g + 1) * gw])
        ht_sc[:, g * gw:(g + 1) * gw] = (ht_prev[:, g * gw:(g + 1) * gw] * cd_exp[:, g * gw:(g + 1) * gw]
                                         + states_t)
        diag = []
        for jp in range(heads_pg // 2):
            ms = []
            for t in range(2):
                h = g * heads_pg + 2 * jp + t
                col = acum[:, SM_DT + h:SM_DT + h + 1]
                rowv = acum_t[SM_DT + h:SM_DT + h + 1, :]
                dec = jnp.exp(jnp.where(causal, col - rowv, NEG_BIG))
                ms.append((cb * dec * dt_t[SM_DT + h:SM_DT + h + 1, :]).astype(BF16))
            mpair = jnp.concatenate(ms, axis=1)
            xpair = xs_b[:, g * gw + jp * LANES:g * gw + (jp + 1) * LANES]
            zero = jnp.zeros_like(xpair)
            xbd = jnp.concatenate([jnp.where(lane < SSD_HEAD_DIM, xpair, zero),
                                   jnp.where(lane >= SSD_HEAD_DIM, xpair, zero)], axis=0)
            diag.append(_dot(mpair, xbd))
        y_parts.append(jnp.concatenate(diag, axis=1) + y_off)
    y = jnp.concatenate(y_parts, axis=1) + dsk_ref[...] * xs

    zval = blk(z_ref)
    if rows < lc:
        zval = jnp.concatenate([zval, jnp.zeros((lc - rows, D_INNER), F32)], axis=0)
    yz = y * _silu(zval)
    outs = []
    for g in range(SSD_GROUPS):
        part = yz[:, g * gw:(g + 1) * gw]
        part = part * lax.rsqrt(jnp.mean(part * part, axis=-1, keepdims=True) + EPS)
        outs.append(part * nrm_ref[:, g * gw:(g + 1) * gw])
    out = jnp.concatenate(outs, axis=1)
    if lead:
        y_ref[0] = out[0:rows].astype(y_ref.dtype)
    else:
        y_ref[...] = out.astype(y_ref.dtype)

    @pl.when(c == pl.num_programs(1) - 1)
    def _():
        for jp in range(SSD_HEADS // 2):
            pair = ht_sc[:, jp * LANES:(jp + 1) * LANES].T
            hout_ref[0, 2 * jp] = pair[0:SSD_HEAD_DIM]
            hout_ref[0, 2 * jp + 1] = pair[SSD_HEAD_DIM:]


def _ssd_consts(p, l):
    pad_l = SM_DT
    pad_r = LANES - SM_DT - SSD_HEADS

    def lanes(v):
        return jnp.pad(v.astype(F32), (pad_l, pad_r)).reshape(1, LANES)

    a_neg = -jnp.exp(p['a_log'][l].astype(F32))
    heads = jnp.arange(D_INNER, dtype=I32) // SSD_HEAD_DIM
    expand = (jnp.arange(LANES, dtype=I32)[:, None] == heads[None, :] + SM_DT).astype(BF16)
    return dict(
        cw=p['conv_w'][l], cb=p['conv_b'][l].reshape(1, CONV_CH),
        dtb=lanes(p['dt_bias'][l]), aneg=lanes(a_neg),
        dsk=jnp.repeat(p['d_skip'][l].astype(F32), SSD_HEAD_DIM).reshape(1, D_INNER),
        nrm=p['norm_ssd'][l].reshape(1, D_INNER), expand=expand)


def _ssd_prompt(ya, yb, bsz, seq, k):
    nc = seq // SSD_CHUNK
    lc = SSD_CHUNK
    const = lambda shape: pl.BlockSpec(shape, lambda b, c: (0,) * len(shape))
    dummy_cs = jnp.zeros((1, 1, CONV_W - 1, CONV_CH), F32)
    dummy_h0 = jnp.zeros((1, 1, 2, SSD_HEAD_DIM, D_STATE), F32)
    return pl.pallas_call(
        functools.partial(_ssd_kernel, rows=lc, has_state=False, lead=False),
        grid=(bsz, nc),
        in_specs=[
            pl.BlockSpec((lc, D_INNER), lambda b, c: (b * nc + c, A_XS // D_INNER)),
            pl.BlockSpec((lc, D_INNER), lambda b, c: (b * nc + c, A_Z // D_INNER)),
            pl.BlockSpec((lc, BC_DIM), lambda b, c: (b * nc + c, A_BM // BC_DIM)),
            pl.BlockSpec((lc, BC_DIM), lambda b, c: (b * nc + c, A_CM // BC_DIM)),
            pl.BlockSpec((lc, LANES), lambda b, c: (b * nc + c, B_SM // LANES)),
            const((1, 1, CONV_W - 1, CONV_CH)),
            const((1, 1, 2, SSD_HEAD_DIM, D_STATE)),
            const((CONV_W, CONV_CH)), const((1, CONV_CH)),
            const((1, LANES)), const((1, LANES)),
            const((1, D_INNER)), const((1, D_INNER)), const((LANES, D_INNER)),
        ],
        out_specs=[pl.BlockSpec((lc, D_INNER), lambda b, c: (b * nc + c, 0)),
                   pl.BlockSpec((1, SSD_HEADS, SSD_HEAD_DIM, D_STATE), lambda b, c: (b, 0, 0, 0))],
        out_shape=[jax.ShapeDtypeStruct((bsz * seq, D_INNER), BF16),
                   jax.ShapeDtypeStruct((bsz, SSD_HEADS, SSD_HEAD_DIM, D_STATE), F32)],
        scratch_shapes=[pltpu.VMEM((lc + 8, CONV_CH), F32), pltpu.VMEM((lc, LANES), F32),
                        pltpu.VMEM((D_STATE, D_INNER), F32)],
        compiler_params=_cparams("parallel", "arbitrary"),
        name="ssd_prompt",
    )(ya, ya, ya, ya, yb, dummy_cs, dummy_h0, k['cw'], k['cb'], k['dtb'], k['aneg'],
      k['dsk'], k['nrm'], k['expand'])


def _ssd_sample(ya3, yb3, conv_state, ssm_state, k, layer):
    bsz, lq, _ = ya3.shape
    lc = SSD_CHUNK
    const = lambda shape: pl.BlockSpec(shape, lambda b, c: (0,) * len(shape))
    return pl.pallas_call(
        functools.partial(_ssd_kernel, rows=lq, has_state=True, lead=True),
        grid=(bsz, 1),
        in_specs=[
            pl.BlockSpec((1, lq, D_INNER), lambda b, c: (b, 0, A_XS // D_INNER)),
            pl.BlockSpec((1, lq, D_INNER), lambda b, c: (b, 0, A_Z // D_INNER)),
            pl.BlockSpec((1, lq, BC_DIM), lambda b, c: (b, 0, A_BM // BC_DIM)),
            pl.BlockSpec((1, lq, BC_DIM), lambda b, c: (b, 0, A_CM // BC_DIM)),
            pl.BlockSpec((1, lq, LANES), lambda b, c: (b, 0, B_SM // LANES)),
            pl.BlockSpec((1, 1, CONV_W - 1, CONV_CH), lambda b, c: (layer, b, 0, 0)),
            pl.BlockSpec((1, 1, SSD_HEADS, SSD_HEAD_DIM, D_STATE),
                         lambda b, c: (layer, b, 0, 0, 0)),
            const((CONV_W, CONV_CH)), const((1, CONV_CH)),
            const((1, LANES)), const((1, LANES)),
            const((1, D_INNER)), const((1, D_INNER)), const((LANES, D_INNER)),
        ],
        out_specs=[pl.BlockSpec((1, lq, D_INNER), lambda b, c: (b, 0, 0)),
                   pl.BlockSpec((1, SSD_HEADS, SSD_HEAD_DIM, D_STATE), lambda b, c: (b, 0, 0, 0))],
        out_shape=[jax.ShapeDtypeStruct((bsz, lq, D_INNER), BF16),
                   jax.ShapeDtypeStruct((bsz, SSD_HEADS, SSD_HEAD_DIM, D_STATE), F32)],
        scratch_shapes=[pltpu.VMEM((lc + 8, CONV_CH), F32), pltpu.VMEM((lc, LANES), F32),
                        pltpu.VMEM((D_STATE, D_INNER), F32)],
        compiler_params=_cparams("parallel", "arbitrary"),
        name="ssd_sample",
    )(ya3, ya3, ya3, ya3, yb3, conv_state, ssm_state, k['cw'], k['cb'], k['dtb'], k['aneg'],
      k['dsk'], k['nrm'], k['expand'])


def _merge_kernel(x_ref, gt_ref, at_ref, sd_ref, ga_ref, gb_ref, wa_ref, wb_ref, wo_ref, o_ref):
    ma = _dot(at_ref[...], wa_ref[...])
    mb = _dot(sd_ref[...], wb_ref[...])
    merged = jax.nn.sigmoid(ga_ref[...]) * ma + jax.nn.sigmoid(gb_ref[...]) * mb
    o_ref[...] = x_ref[...] + gt_ref[0] * _dot(merged.astype(BF16), wo_ref[...])


def _merge(x, gt, attn, ssd, ya, wa, wb, wo, *, tm):
    t, d = x.shape
    groups, r, _ = gt.shape
    tpg = (t // groups) // tm
    return pl.pallas_call(
        _merge_kernel,
        grid=(t // tm,),
        in_specs=[
            pl.BlockSpec((tm, d), lambda i: (i, 0)),
            pl.BlockSpec((1, r, d), lambda i: (i // tpg, 0, 0)),
            pl.BlockSpec((tm, ATT_DIM), lambda i: (i, 0)),
            pl.BlockSpec((tm, D_INNER), lambda i: (i, 0)),
            pl.BlockSpec((tm, d), lambda i: (i, A_GA // D_MODEL)),
            pl.BlockSpec((tm, d), lambda i: (i, A_GB // D_MODEL)),
            pl.BlockSpec((ATT_DIM, d), lambda i: (0, 0)),
            pl.BlockSpec((D_INNER, d), lambda i: (0, 0)),
            pl.BlockSpec((d, d), lambda i: (0, 0)),
        ],
        out_specs=pl.BlockSpec((tm, d), lambda i: (i, 0)),
        out_shape=jax.ShapeDtypeStruct((t, d), F32),
        compiler_params=_cparams("parallel"),
        name="merge",
    )(x, gt, attn, ssd, ya, ya, wa, wb, wo)


def _layer_weights(p, l):
    w_in = p['w_in'][l]
    offs = {}
    o = 0
    for name, size in (('q', ATT_DIM), ('k', KV_DIM), ('v', KV_DIM), ('iq', IDX_HEADS * IDX_DIM),
                       ('ik', IDX_DIM), ('iw', IDX_HEADS), ('z', D_INNER), ('xbc', CONV_CH),
                       ('dt', SSD_HEADS), ('ga', D_MODEL), ('gb', D_MODEL)):
        offs[name] = (o, o + size)
        o += size

    def seg(name):
        s, e = offs[name]
        return w_in[:, s:e]

    xbc = seg('xbc')
    w_a = jnp.concatenate([xbc[:, :D_INNER], seg('z'), seg('ga'), seg('gb'),
                           xbc[:, D_INNER:D_INNER + BC_DIM], xbc[:, D_INNER + BC_DIM:]], axis=1)
    small = jnp.concatenate([seg('ik'), seg('iw'), seg('dt')], axis=1)
    small = jnp.pad(small, ((0, 0), (0, B_DIM - B_SM - small.shape[1])))
    w_b = jnp.concatenate([seg('q'), seg('k'), seg('v'), seg('iq'), small], axis=1)
    return dict(
        w_a=w_a.astype(BF16), w_b=w_b.astype(BF16),
        up1=p['w_ffn1_up'][l].astype(BF16), down1=p['w_ffn1_down'][l].astype(BF16),
        up2=p['w_ffn2_up'][l].astype(BF16), down2=p['w_ffn2_down'][l].astype(BF16),
        wa=p['w_br_a'][l].astype(BF16), wb=p['w_br_b'][l].astype(BF16), wo=p['w_out'][l].astype(BF16),
        g1=p['norm_ffn1'][l].reshape(1, D_MODEL), gm=p['norm_mix'][l].reshape(1, D_MODEL),
        g2=p['norm_ffn2'][l].reshape(1, D_MODEL))


def _pre_mix(x, mods, w, tables, tm):
    sh1, sc1, g1, sh2, sc2 = mods[0], mods[1], mods[2], mods[3], mods[4]
    x = _ffn(x, sh1, sc1, g1, w['g1'], w['g1'], w['up1'], w['down1'], tm=tm, final_norm=False)
    ya = _proj_a(x, sh2, sc2, w['gm'], w['w_a'], tm=tm)
    yb, ybh = _proj_b(x, sh2, sc2, w['gm'], w['w_b'], tables[0], tables[1], tm=tm)
    return x, ya, yb, ybh


def _post_mix(x, mods, w, attn, ssd, ya, gf, tm, tm_merge, final_norm):
    x = _merge(x, mods[5], attn, ssd, ya, w['wa'], w['wb'], w['wo'], tm=tm_merge)
    return _ffn(x, mods[6], mods[7], mods[8], w['g2'], gf, w['up2'], w['down2'],
                tm=tm, final_norm=final_norm)


def kernel(x_prompt, x_sample, cache_k, cache_v, cache_idx_k, state_ssm, state_conv, page_table,
           c_prompt, c_sample, w_ada, b_ada, norm_ffn1, w_ffn1_up, w_ffn1_down, norm_mix, w_in,
           conv_w, conv_b, dt_bias, a_log, d_skip, norm_ssd, w_br_a, w_br_b, w_out,
           norm_ffn2, w_ffn2_up, w_ffn2_down, norm_final):
    p = {'w_ada': w_ada, 'b_ada': b_ada, 'norm_ffn1': norm_ffn1, 'w_ffn1_up': w_ffn1_up,
         'w_ffn1_down': w_ffn1_down, 'norm_mix': norm_mix, 'w_in': w_in, 'conv_w': conv_w,
         'conv_b': conv_b, 'dt_bias': dt_bias, 'a_log': a_log, 'd_skip': d_skip,
         'norm_ssd': norm_ssd, 'w_br_a': w_br_a, 'w_br_b': w_br_b, 'w_out': w_out,
         'norm_ffn2': norm_ffn2, 'w_ffn2_up': w_ffn2_up, 'w_ffn2_down': w_ffn2_down}
    depth = w_in.shape[0]
    bsz, seq, d = x_prompt.shape
    dbsz, dseq, _ = x_sample.shape
    n_pages = page_table.shape[1]
    past_len = n_pages * PAGE_SIZE
    n_pool = cache_k.shape[1]
    tp = bsz * seq
    ts = dbsz * dseq
    tm_p = min(1024, seq)
    tm_merge = min(256, seq)

    rows = bsz + dbsz
    rpad = -rows % 8
    c_all = jnp.pad(jnp.concatenate([c_prompt, c_sample], axis=0), ((0, rpad), (0, 0)))
    mods_all = _ada(c_all, w_ada, b_ada)

    def mods_for(l):
        mp, ms = [], []
        for i in range(N_MODS):
            m = mods_all[l, :, i * d:(i + 1) * d]
            mp.append(m[:bsz].reshape(bsz, 1, d))
            ms.append(jnp.repeat(m[bsz:rows], dseq, axis=0).reshape(1, ts, d))
        return mp, ms

    tab_p = _rope_tables(jnp.arange(seq, dtype=I32))
    tab_s = _rope_tables(past_len + jnp.tile(jnp.arange(dseq, dtype=I32), dbsz))
    gf = norm_final.reshape(1, d)
    n_top_p = min(INDEX_TOPK, seq // 4)
    n_top_s = min(INDEX_TOPK, (past_len + dseq) // 4)

    pool_k = cache_k.reshape(depth, n_pool, PAGE_SIZE * KV_HEADS, HEAD_DIM)
    pool_v = cache_v.reshape(depth, n_pool, PAGE_SIZE * KV_HEADS, HEAD_DIM)
    pool_ik_t = jnp.swapaxes(cache_idx_k, 2, 3)

    xp = x_prompt.reshape(tp, d)
    xs = x_sample.reshape(ts, d)
    outs = {k: [] for k in ('kp', 'vp', 'ikp', 'sp', 'cp', 'ks', 'vs', 'iks', 'ss', 'cs')}
    for l in range(depth):
        w = _layer_weights(p, l)
        kc = _ssd_consts(p, l)
        mp, ms = mods_for(l)
        last = l == depth - 1

        xp, ya, yb, ybh = _pre_mix(xp, mp, w, tab_p, tm_p)
        attn = _dsa_prompt(yb, ybh, bsz, seq, n_top_p)
        ssd, h_fin = _ssd_prompt(ya, yb, bsz, seq, kc)
        xp = _post_mix(xp, mp, w, attn, ssd, ya, gf, tm_p, tm_merge, last)
        yb4 = yb.reshape(bsz, seq, B_DIM)
        ya4 = ya.reshape(bsz, seq, A_DIM)
        outs['kp'].append(yb4[:, :, B_K:B_K + KV_DIM].reshape(bsz, seq, KV_HEADS, HEAD_DIM))
        outs['vp'].append(yb4[:, :, B_V:B_V + KV_DIM].reshape(bsz, seq, KV_HEADS, HEAD_DIM))
        outs['ikp'].append(yb4[:, :, B_SM:B_SM + IDX_DIM])
        outs['sp'].append(h_fin)
        tail = ya4[:, seq - (CONV_W - 1):, :]
        outs['cp'].append(jnp.concatenate(
            [tail[:, :, A_XS:A_XS + D_INNER], tail[:, :, A_BM:A_BM + BC_DIM],
             tail[:, :, A_CM:A_CM + BC_DIM]], axis=-1))

        xs, ya, yb, _ = _pre_mix(xs, ms, w, tab_s, ts)
        ya3 = ya.reshape(dbsz, dseq, A_DIM)
        yb3 = yb.reshape(dbsz, dseq, B_DIM)
        rpad_s = PAGE_SIZE - dseq
        sm_pad = jnp.pad(yb3[:, :, B_SM:B_SM + LANES], ((0, 0), (0, rpad_s), (0, 0)))
        page_rows = PAGE_SIZE * KV_HEADS
        kn_pad = jnp.pad(yb3[:, :, B_K:B_K + KV_DIM], ((0, 0), (0, rpad_s), (0, 0)))
        vn_pad = jnp.pad(yb3[:, :, B_V:B_V + KV_DIM], ((0, 0), (0, rpad_s), (0, 0)))
        kn_pad = kn_pad.reshape(dbsz, 1, page_rows, HEAD_DIM)
        vn_pad = vn_pad.reshape(dbsz, 1, page_rows, HEAD_DIM)
        s_past, s_new = _sample_scores(page_table, yb3, sm_pad, pool_ik_t, l)
        s_fill = jnp.full((dbsz, dseq, SEL_CK - PAGE_SIZE), -jnp.inf, F32)
        scores = jnp.concatenate([s_past, s_new, s_fill], axis=-1).reshape(ts, past_len + SEL_CK)
        bias = _sample_select(scores, n_top_s).reshape(dbsz, dseq, past_len + SEL_CK)
        attn = _sample_attn(page_table, yb3, bias, kn_pad, vn_pad, pool_k, pool_v, l)
        ssd, h_fin = _ssd_sample(ya3, yb3, state_conv, state_ssm, kc, l)
        xs = _post_mix(xs, ms, w, attn.reshape(ts, ATT_DIM), ssd.reshape(ts, D_INNER), ya, gf,
                       ts, ts, last)
        outs['ks'].append(yb3[:, :, B_K:B_K + KV_DIM].reshape(dbsz, dseq, KV_HEADS, HEAD_DIM))
        outs['vs'].append(yb3[:, :, B_V:B_V + KV_DIM].reshape(dbsz, dseq, KV_HEADS, HEAD_DIM))
        outs['iks'].append(yb3[:, :, B_SM:B_SM + IDX_DIM])
        outs['ss'].append(h_fin)
        xbc = jnp.concatenate([ya3[:, :, A_XS:A_XS + D_INNER], ya3[:, :, A_BM:A_BM + BC_DIM],
                               ya3[:, :, A_CM:A_CM + BC_DIM]], axis=-1)
        xpad = jnp.concatenate([state_conv[l], xbc], axis=1)
        outs['cs'].append(xpad[:, dseq:])

    return (xp.reshape(bsz, seq, d), xs.reshape(dbsz, dseq, d),
            jnp.stack(outs['kp']), jnp.stack(outs['vp']), jnp.stack(outs['ikp']),
            jnp.stack(outs['sp']), jnp.stack(outs['cp']),
            jnp.stack(outs['ks']), jnp.stack(outs['vs']), jnp.stack(outs['iks']),
            jnp.stack(outs['ss']), jnp.stack(outs['cs']))
```

```python
import functools
import math

import jax
import jax.numpy as jnp
from jax import lax
from jax.experimental import pallas as pl
from jax.experimental.pallas import tpu as pltpu

F32 = jnp.float32
BF16 = jnp.bfloat16
I32 = jnp.int32

D_MODEL = 1024
PAGE_SIZE = 128
N_HEADS = 8
HEAD_DIM = 128
KV_HEADS = 4
Q_PER_KV = N_HEADS // KV_HEADS
ATT_DIM = N_HEADS * HEAD_DIM
KV_DIM = KV_HEADS * HEAD_DIM
ATT_SCALE = HEAD_DIM ** -0.5
IDX_HEADS = 8
IDX_DIM = 64
IDX_ROPE_DIM = 32
IDX_SCALE = IDX_HEADS ** -0.5 * IDX_DIM ** -0.5
INDEX_TOPK = 256
ROPE_THETA = 10000.0
D_INNER = 2 * D_MODEL
SSD_HEAD_DIM = 64
SSD_HEADS = D_INNER // SSD_HEAD_DIM
SSD_GROUPS = 4
D_STATE = 128
CONV_W = 4
BC_DIM = SSD_GROUPS * D_STATE
CONV_CH = D_INNER + 2 * BC_DIM
SSD_CHUNK = 128
FFN_DIM = 2816
N_MODS = 9
EPS = 1e-6

LANES = 128
NEG_BIG = -1e30
INT_MIN = -(2 ** 31)
NEG_INF_KEY = -2139095041
VMEM_LIMIT = 56 * 1024 * 1024

A_XS, A_Z, A_GA, A_GB, A_BM, A_CM = 0, 2048, 4096, 5120, 6144, 6656
A_DIM = 7168
B_Q, B_K, B_V, B_IQ, B_SM = 0, 1024, 1536, 2048, 2560
B_DIM = 3072
SM_IW = IDX_DIM
SM_DT = IDX_DIM + IDX_HEADS
PROJ_TN = 512


def _cparams(*sem):
    return pltpu.CompilerParams(dimension_semantics=sem, vmem_limit_bytes=VMEM_LIMIT)


def _silu(v):
    return v * jax.nn.sigmoid(v)


def _norm_mod(x, g, shift, scale):
    xn = x * lax.rsqrt(jnp.mean(x * x, axis=-1, keepdims=True) + EPS) * g
    return xn * (1.0 + scale) + shift


def _dot(a, b):
    return jnp.dot(a, b, preferred_element_type=F32)


def _dot_nt(a, b):
    return lax.dot_general(a, b, (((1,), (1,)), ((), ())), preferred_element_type=F32)


def _ada_kernel(c_ref, w_ref, b_ref, o_ref):
    s = _silu(c_ref[...]).astype(BF16)
    o_ref[0] = _dot(s, w_ref[0].astype(BF16)) + b_ref[0]


def _ada(c_all, w_ada, b_ada):
    depth, d, n = w_ada.shape
    r = c_all.shape[0]
    tn = 1024
    return pl.pallas_call(
        _ada_kernel,
        grid=(depth, n // tn),
        in_specs=[
            pl.BlockSpec((r, d), lambda l, j: (0, 0)),
            pl.BlockSpec((1, d, tn), lambda l, j: (l, 0, j)),
            pl.BlockSpec((1, 1, tn), lambda l, j: (l, 0, j)),
        ],
        out_specs=pl.BlockSpec((1, r, tn), lambda l, j: (l, 0, j)),
        out_shape=jax.ShapeDtypeStruct((depth, r, n), F32),
        compiler_params=_cparams("parallel", "parallel"),
        name="adaln",
    )(c_all, w_ada, b_ada.reshape(depth, 1, n))


def _resident(shape):
    return pl.BlockSpec(shape, lambda i: (0,) * len(shape), pipeline_mode=pl.Buffered(1))


def _mod_spec(mod, t, tm):
    groups, r, d = mod.shape
    tpg = (t // groups) // tm
    return pl.BlockSpec((1, r, d), lambda i: (i // tpg, 0, 0))


FFN_TF = 2816


def _ffn_kernel(x_ref, sh_ref, sc_ref, gt_ref, g_ref, gf_ref, wu_ref, wd_ref, o_ref, *, final_norm):
    x = x_ref[...]
    h = _norm_mod(x, g_ref[...], sh_ref[0], sc_ref[0]).astype(BF16)
    f = wd_ref.shape[0]
    acc = None
    for j in range(f // FFN_TF):
        a = _dot(h, wu_ref[:, j * FFN_TF:(j + 1) * FFN_TF])
        b = _dot(h, wu_ref[:, f + j * FFN_TF:f + (j + 1) * FFN_TF])
        t = (_silu(a) * b).astype(BF16)
        part = _dot(t, wd_ref[j * FFN_TF:(j + 1) * FFN_TF, :])
        acc = part if acc is None else acc + part
    y = x + 0.5 * gt_ref[0] * acc
    if final_norm:
        y = y * lax.rsqrt(jnp.mean(y * y, axis=-1, keepdims=True) + EPS) * gf_ref[...]
    o_ref[...] = y


def _ffn(x, sh, sc, gt, g, gf, w_up, w_down, *, tm, final_norm):
    t, d = x.shape
    mod = _mod_spec(sh, t, tm)
    return pl.pallas_call(
        functools.partial(_ffn_kernel, final_norm=final_norm),
        grid=(t // tm,),
        in_specs=[
            pl.BlockSpec((tm, d), lambda i: (i, 0)),
            mod, mod, mod,
            _resident((1, d)), _resident((1, d)),
            _resident(w_up.shape), _resident(w_down.shape),
        ],
        out_specs=pl.BlockSpec((tm, d), lambda i: (i, 0)),
        out_shape=jax.ShapeDtypeStruct((t, d), F32),
        compiler_params=_cparams("parallel"),
        name="ffn",
    )(x, sh, sc, gt, g, gf, w_up, w_down)


def _proj_a_kernel(x_ref, sh_ref, sc_ref, g_ref, w_ref, o_ref):
    h = _norm_mod(x_ref[...], g_ref[...], sh_ref[0], sc_ref[0]).astype(BF16)
    for j in range(w_ref.shape[1] // PROJ_TN):
        cols = slice(j * PROJ_TN, (j + 1) * PROJ_TN)
        o_ref[:, cols] = _dot(h, w_ref[:, cols])


def _proj_b_kernel(x_ref, sh_ref, sc_ref, g_ref, w_ref, t64_ref, t16_ref, o_ref, ob_ref):
    h = _norm_mod(x_ref[...], g_ref[...], sh_ref[0], sc_ref[0]).astype(BF16)
    half = IDX_ROPE_DIM // 2

    def emit(col, v):
        o_ref[:, col:col + LANES] = v
        ob_ref[:, col:col + LANES] = v.astype(BF16)

    for j in range(w_ref.shape[1] // PROJ_TN):
        base = j * PROJ_TN
        y = _dot(h, w_ref[:, base:base + PROJ_TN])
        for g in range(PROJ_TN // LANES):
            col = base + g * LANES
            yg = y[:, g * LANES:(g + 1) * LANES]
            if col < B_V:
                tab = 0 if col < B_K else 1
                c = t64_ref[tab, :, 0:LANES]
                a = t64_ref[tab, :, LANES:2 * LANES]
                emit(col, yg * c + pltpu.roll(yg, HEAD_DIM // 2, 1) * a)
            elif col < B_IQ:
                emit(col, yg)
            else:
                tab = 0 if col < B_SM else 1
                c = t16_ref[tab, :, 0:LANES]
                a = t16_ref[tab, :, LANES:2 * LANES]
                b = t16_ref[tab, :, 2 * LANES:3 * LANES]
                emit(col, yg * c + pltpu.roll(yg, half, 1) * a
                     + pltpu.roll(yg, LANES - half, 1) * b)


def _proj_a(x, sh, sc, g, w, *, tm):
    t, d = x.shape
    n = w.shape[1]
    mod = _mod_spec(sh, t, tm)
    return pl.pallas_call(
        _proj_a_kernel,
        grid=(t // tm,),
        in_specs=[
            pl.BlockSpec((tm, d), lambda i: (i, 0)),
            mod, mod,
            _resident((1, d)), _resident(w.shape),
        ],
        out_specs=pl.BlockSpec((tm, n), lambda i: (i, 0)),
        out_shape=jax.ShapeDtypeStruct((t, n), F32),
        compiler_params=_cparams("parallel"),
        name="proj_a",
    )(x, sh, sc, g, w)


def _proj_b(x, sh, sc, g, w, t64, t16, *, tm):
    t, d = x.shape
    n = w.shape[1]
    mod = _mod_spec(sh, t, tm)
    npos = t64.shape[1] // tm
    return pl.pallas_call(
        _proj_b_kernel,
        grid=(t // tm,),
        in_specs=[
            pl.BlockSpec((tm, d), lambda i: (i, 0)),
            mod, mod,
            _resident((1, d)), _resident(w.shape),
            pl.BlockSpec((2, tm, 2 * LANES), lambda i: (0, i % npos, 0)),
            pl.BlockSpec((2, tm, 3 * LANES), lambda i: (0, i % npos, 0)),
        ],
        out_specs=[pl.BlockSpec((tm, n), lambda i: (i, 0)),
                   pl.BlockSpec((tm, n), lambda i: (i, 0))],
        out_shape=[jax.ShapeDtypeStruct((t, n), F32), jax.ShapeDtypeStruct((t, n), BF16)],
        compiler_params=_cparams("parallel"),
        name="proj_b",
    )(x, sh, sc, g, w, t64, t16)


def _rope_tables(pos):
    posf = pos.astype(F32)[:, None]
    half = HEAD_DIM // 2
    inv = ROPE_THETA ** (-jnp.arange(half, dtype=F32) / half)
    ang = posf * inv[None, :]
    cos, sin = jnp.cos(ang), jnp.sin(ang)
    t_k = jnp.concatenate([cos, cos, -sin, sin], axis=1)
    t64 = jnp.stack([t_k * ATT_SCALE, t_k])
    ih = IDX_ROPE_DIM // 2
    inv_i = ROPE_THETA ** (-jnp.arange(ih, dtype=F32) / ih)
    ang_i = posf * inv_i[None, :]
    ci, si = jnp.cos(ang_i), jnp.sin(ang_i)
    n = pos.shape[0]
    ones = jnp.ones((n, IDX_DIM - IDX_ROPE_DIM), F32)
    zeros = jnp.zeros((n, IDX_DIM - IDX_ROPE_DIM), F32)
    zh = jnp.zeros((n, ih), F32)
    c_head = jnp.concatenate([ci, ci, ones], axis=1)
    a_head = jnp.concatenate([zh, si, zeros], axis=1)
    b_head = jnp.concatenate([-si, zh, zeros], axis=1)
    t_iq = jnp.concatenate([c_head, c_head, a_head, a_head, b_head, b_head], axis=1)
    z64 = jnp.zeros((n, IDX_DIM), F32)
    c_tail = jnp.concatenate([jnp.full((n, IDX_HEADS), IDX_SCALE, F32),
                              jnp.ones((n, LANES - IDX_DIM - IDX_HEADS), F32)], axis=1)
    t_sm = jnp.concatenate([c_head, c_tail, a_head, z64, b_head, z64], axis=1)
    return t64, jnp.stack([t_iq, t_sm])


def _sortable(x):
    b = lax.bitcast_convert_type(x, I32)
    b = jnp.where(b == INT_MIN, 0, b)
    return jnp.where(b < 0, b ^ 0x7FFFFFFF, b)


def _select_topk(key_sc, bias_sc, nk, n_top, idx_bits):
    _, rows, ck = key_sc.shape
    nblk = ck // LANES
    rb = min(rows, 128)

    def col_index(nrows, c, t):
        return lax.broadcasted_iota(I32, (nrows, LANES), 1) + (c * ck + t * LANES)

    def lanes(rep, rs):
        return rep[rs]

    def row_sum(acc):
        return jnp.broadcast_to(jnp.sum(acc, axis=1, keepdims=True), acc.shape)

    def count(make_pred):
        slices = [slice(r * rb, (r + 1) * rb) for r in range(rows // rb)]
        preds = [make_pred(rs) for rs in slices]
        accs = []
        for rs, pred in zip(slices, preds):
            def body(c, acc, rs=rs, pred=pred):
                for t in range(nblk):
                    k = key_sc[c, rs, t * LANES:(t + 1) * LANES]
                    acc = acc + jnp.where(pred(k, c, t), 1.0, 0.0)
                return acc
            accs.append(lax.fori_loop(0, nk, body, jnp.zeros((rb, LANES), F32)))
        acc = accs[0] if len(accs) == 1 else jnp.concatenate(accs, axis=0)
        return row_sum(acc)

    def ge(col):
        def make(rs):
            cb = lanes(col, rs)
            return lambda k, c, t: k >= cb
        return make

    def gt(col):
        def make(rs):
            cb = lanes(col, rs)
            return lambda k, c, t: k > cb
        return make

    k_top = float(n_top)

    def vstep(it, lo):
        cand = lo + lax.shift_left(jnp.int32(1), 31 - it)
        cnt = count(ge(cand))
        return jnp.where(cnt >= k_top, cand, lo)

    lo = lax.fori_loop(0, 32, vstep, jnp.full((rows, LANES), INT_MIN, I32))
    cnt_gt = count(gt(lo))
    cnt_ge = count(ge(lo))
    need = k_top - cnt_gt
    excess = (cnt_ge > k_top) & (lo > NEG_INF_KEY)
    any_excess = jnp.max(excess.astype(I32)) > 0

    @pl.when(jnp.logical_not(any_excess))
    def _():
        thr = jnp.maximum(lo, NEG_INF_KEY + 1)

        def wbody(c, carry):
            for t in range(nblk):
                sl = slice(t * LANES, (t + 1) * LANES)
                bias_sc[c, :, sl] = jnp.where(key_sc[c, :, sl] >= thr, 0.0, NEG_BIG)
            return carry

        lax.fori_loop(0, nk, wbody, 0)

    @pl.when(any_excess)
    def _():
        def tie_below(cand):
            def make(rs):
                lob, cb = lanes(lo, rs), lanes(cand, rs)
                return lambda k, c, t: (k == lob) & (col_index(rb, c, t) < cb)
            return make

        def jstep(it, m):
            cand = m + lax.shift_left(jnp.int32(1), idx_bits - 1 - it)
            f = count(tie_below(cand))
            return jnp.where(f < need, cand, m)
        m = lax.fori_loop(0, idx_bits, jstep, jnp.zeros((rows, LANES), I32))
        jmax = jnp.where(excess, m, jnp.int32(2 ** 30))

        def wbody(c, carry):
            for t in range(nblk):
                sl = slice(t * LANES, (t + 1) * LANES)
                k = key_sc[c, :, sl]
                sel = (k > lo) | ((k == lo) & (col_index(rows, c, t) <= jmax))
                sel = sel & (k > NEG_INF_KEY)
                bias_sc[c, :, sl] = jnp.where(sel, 0.0, NEG_BIG)
            return carry

        lax.fori_loop(0, nk, wbody, 0)


def _select_topk_cols(key_sc, bias_sc, nk, n_top, idx_bits):
    _, ck, nq = key_sc.shape
    sub = 8
    nslab = ck // sub
    nacc = 4
    k_top = float(n_top)

    def key_index(c, r):
        return lax.broadcasted_iota(I32, (sub, nq), 0) + (c * ck + r * sub)

    def count(pred):
        def body(c, accs):
            accs = list(accs)
            for r in range(nslab):
                k = key_sc[c, r * sub:(r + 1) * sub, :]
                accs[r % nacc] = accs[r % nacc] + jnp.where(pred(k, c, r), 1.0, 0.0)
            return tuple(accs)
        accs = lax.fori_loop(0, nk, body, tuple(jnp.zeros((sub, nq), F32) for _ in range(nacc)))
        acc = (accs[0] + accs[1]) + (accs[2] + accs[3])
        return jnp.broadcast_to(jnp.sum(acc, axis=0, keepdims=True), (sub, nq))

    def vstep(it, lo):
        cand = lo + lax.shift_left(jnp.int32(1), 31 - it)
        cnt = count(lambda k, c, r: k >= cand)
        return jnp.where(cnt >= k_top, cand, lo)

    lo = lax.fori_loop(0, 32, vstep, jnp.full((sub, nq), INT_MIN, I32))
    cnt_gt = count(lambda k, c, r: k > lo)
    cnt_ge = count(lambda k, c, r: k >= lo)
    need = k_top - cnt_gt
    excess = (cnt_ge > k_top) & (lo > NEG_INF_KEY)
    any_excess = jnp.max(excess.astype(I32)) > 0
    eye = (lax.broadcasted_iota(I32, (LANES, LANES), 0)
           == lax.broadcasted_iota(I32, (LANES, LANES), 1)).astype(BF16)

    def write(c, selected):
        for qb in range(nq // LANES):
            qs = slice(qb * LANES, (qb + 1) * LANES)
            sel = selected(key_sc[c, :, qs], qb)
            bias_t = jnp.where(sel, 0.0, NEG_BIG).astype(BF16)
            bias_sc[c, qs, :] = _dot_nt(eye, bias_t)

    @pl.when(jnp.logical_not(any_excess))
    def _():
        thr = jnp.maximum(lo, NEG_INF_KEY + 1)[0:1, :]

        def wbody(c, carry):
            write(c, lambda k, qb: k >= thr[:, qb * LANES:(qb + 1) * LANES])
            return carry

        lax.fori_loop(0, nk, wbody, 0)

    @pl.when(any_excess)
    def _():
        def jstep(it, m):
            cand = m + lax.shift_left(jnp.int32(1), idx_bits - 1 - it)
            f = count(lambda k, c, r: (k == lo) & (key_index(c, r) < cand))
            return jnp.where(f < need, cand, m)
        m = lax.fori_loop(0, idx_bits, jstep, jnp.zeros((sub, nq), I32))
        jmax = jnp.where(excess, m, jnp.int32(2 ** 30))[0:1, :]
        lo1 = lo[0:1, :]

        def wbody(c, carry):
            pos = lax.broadcasted_iota(I32, (ck, LANES), 0) + c * ck

            def selected(k, qb):
                qs = slice(qb * LANES, (qb + 1) * LANES)
                sel = (k > lo1[:, qs]) | ((k == lo1[:, qs]) & (pos <= jmax[:, qs]))
                return sel & (k > NEG_INF_KEY)
            write(c, selected)
            return carry

        lax.fori_loop(0, nk, wbody, 0)


def _flash_step(s, v, m_prev, l_prev, acc_prev):
    nblk = s.shape[1] // LANES
    m_new = jnp.maximum(m_prev, jnp.max(s, axis=1, keepdims=True))
    alpha = jnp.exp(m_prev - m_new)
    p = [jnp.exp(s[:, t * LANES:(t + 1) * LANES] - m_new) for t in range(nblk)]
    psum = p[0]
    for t in range(1, nblk):
        psum = psum + p[t]
    l_new = alpha * l_prev + jnp.sum(psum, axis=1, keepdims=True)
    pb = jnp.concatenate([pt.astype(BF16) for pt in p], axis=1)
    acc_new = alpha * acc_prev + _dot(pb, v)
    return m_new, l_new, acc_new


ATT_CK = 512


def _dsa_prompt_kernel(q_ref, iq_ref, smq_ref, k_ref, v_ref, smk_ref, o_ref,
                       key_sc, bias_sc, m_sc, l_sc, acc_sc, *, n_top, idx_bits):
    qi = pl.program_id(1)
    tq, ck = q_ref.shape[0], ATT_CK
    nk = (qi * tq + tq + ck - 1) // ck
    lane = lax.broadcasted_iota(I32, (tq, LANES), 1)

    iq = iq_ref[...].astype(F32)
    iqh = []
    for p in range(IDX_HEADS // 2):
        grp = iq[:, p * LANES:(p + 1) * LANES]
        iqh.append(jnp.where(lane < IDX_DIM, grp, 0.0).astype(BF16))
        iqh.append(jnp.where(lane < IDX_DIM, pltpu.roll(grp, IDX_DIM, 1), 0.0).astype(BF16))
    smq = smq_ref[...]
    smq_t = jnp.concatenate([smq[b * LANES:(b + 1) * LANES].T for b in range(tq // LANES)], axis=1)
    qpos = qi * tq + lax.broadcasted_iota(I32, (ck, tq), 1)

    def score_body(c, carry):
        kc = smk_ref[pl.ds(pl.multiple_of(c * ck, ck), ck), :]
        sc = jnp.zeros((ck, tq), F32)
        for h in range(IDX_HEADS):
            lg = _dot_nt(kc, iqh[h])
            sc = sc + jnp.maximum(lg, 0.0) * smq_t[SM_IW + h:SM_IW + h + 1, :]
        kpos = c * ck + lax.broadcasted_iota(I32, (ck, tq), 0)
        sc = jnp.where(kpos <= qpos, sc, -jnp.inf)
        key_sc[c] = _sortable(sc)
        return carry

    lax.fori_loop(0, nk, score_body, 0)
    _select_topk_cols(key_sc, bias_sc, nk, n_top, idx_bits)

    m_sc[...] = jnp.full(m_sc.shape, NEG_BIG, F32)
    l_sc[...] = jnp.zeros_like(l_sc)
    acc_sc[...] = jnp.zeros_like(acc_sc)

    def att_body(c, carry):
        off = pl.multiple_of(c * ck, ck)
        bias = bias_sc[c]
        bias2 = jnp.concatenate([bias] * Q_PER_KV, axis=0)
        new = []
        for g in range(KV_HEADS):
            q2 = jnp.concatenate(
                [q_ref[:, (Q_PER_KV * g + t) * HEAD_DIM:(Q_PER_KV * g + t + 1) * HEAD_DIM]
                 for t in range(Q_PER_KV)], axis=0)
            kc = k_ref[pl.ds(off, ck), g * HEAD_DIM:(g + 1) * HEAD_DIM]
            vc = v_ref[pl.ds(off, ck), g * HEAD_DIM:(g + 1) * HEAD_DIM]
            s = _dot_nt(q2, kc) + bias2
            new.append(_flash_step(s, vc, m_sc[g], l_sc[g], acc_sc[g]))
        for g, (m_new, l_new, acc_new) in enumerate(new):
            m_sc[g] = m_new
            l_sc[g] = l_new
            acc_sc[g] = acc_new
        return carry

    lax.fori_loop(0, nk, att_body, 0)
    for g in range(KV_HEADS):
        o = acc_sc[g] / l_sc[g]
        for t in range(Q_PER_KV):
            h = Q_PER_KV * g + t
            o_ref[:, h * HEAD_DIM:(h + 1) * HEAD_DIM] = o[t * tq:(t + 1) * tq].astype(o_ref.dtype)


def _dsa_prompt(yb, ybh, bsz, seq, n_top):
    tq = min(256, seq)
    nq = seq // tq
    nch = seq // ATT_CK
    kernel = functools.partial(_dsa_prompt_kernel, n_top=n_top,
                               idx_bits=max(1, (seq - 1).bit_length()))
    return pl.pallas_call(
        kernel,
        grid=(bsz, nq),
        in_specs=[
            pl.BlockSpec((tq, ATT_DIM), lambda b, i: (b * nq + i, B_Q // ATT_DIM)),
            pl.BlockSpec((tq, 512), lambda b, i: (b * nq + i, B_IQ // 512)),
            pl.BlockSpec((tq, LANES), lambda b, i: (b * nq + i, B_SM // LANES)),
            pl.BlockSpec((seq, KV_DIM), lambda b, i: (b, B_K // KV_DIM)),
            pl.BlockSpec((seq, KV_DIM), lambda b, i: (b, B_V // KV_DIM)),
            pl.BlockSpec((seq, LANES), lambda b, i: (b, B_SM // LANES)),
        ],
        out_specs=pl.BlockSpec((tq, ATT_DIM), lambda b, i: (b * nq + i, 0)),
        out_shape=jax.ShapeDtypeStruct((bsz * seq, ATT_DIM), BF16),
        scratch_shapes=[pltpu.VMEM((nch, ATT_CK, tq), I32),
                        pltpu.VMEM((nch, tq, ATT_CK), F32),
                        pltpu.VMEM((KV_HEADS, Q_PER_KV * tq, LANES), F32),
                        pltpu.VMEM((KV_HEADS, Q_PER_KV * tq, LANES), F32),
                        pltpu.VMEM((KV_HEADS, Q_PER_KV * tq, HEAD_DIM), F32)],
        compiler_params=_cparams("parallel", "arbitrary"),
        name="dsa_prompt",
    )(ybh, ybh, yb, ybh, ybh, ybh)


PAGES_PER_STEP = 8
SCORE_PAGES_PER_STEP = 16


def _sample_score_kernel(pt_ref, iq_ref, sm_ref, smn_ref, *refs, npg):
    del pt_ref
    ik_refs = refs[:npg]
    sp_ref, sn_ref = refs[npg], refs[npg + 1]
    lq = iq_ref.shape[1]
    iq = iq_ref[0]
    sm = sm_ref[0]
    iqh = jnp.concatenate([iq[:, h * IDX_DIM:(h + 1) * IDX_DIM] for h in range(IDX_HEADS)],
                          axis=0).astype(BF16)

    def score(lg):
        sc = jnp.zeros((lq, lg.shape[1]), F32)
        for h in range(IDX_HEADS):
            sc = sc + jnp.maximum(lg[h * lq:(h + 1) * lq], 0.0) * sm[:, SM_IW + h:SM_IW + h + 1]
        return sc

    for i in range(npg):
        sp_ref[0, :, i * PAGE_SIZE:(i + 1) * PAGE_SIZE] = score(
            _dot(iqh, ik_refs[i][0, 0].astype(BF16)))

    @pl.when(pl.program_id(1) == 0)
    def _():
        sc = score(_dot_nt(iqh, smn_ref[0][:, 0:IDX_DIM].astype(BF16)))
        col = lax.broadcasted_iota(I32, sc.shape, 1)
        row = lax.broadcasted_iota(I32, sc.shape, 0)
        sn_ref[0] = jnp.where(col <= row, sc, -jnp.inf)


def _sample_scores(page_table, yb3, sm_pad, pool_ik_t, layer):
    bsz, lq, _ = yb3.shape
    n_pages = page_table.shape[1]
    npg = math.gcd(SCORE_PAGES_PER_STEP, n_pages)
    steps = n_pages // npg
    ik_specs = [pl.BlockSpec((1, 1, IDX_DIM, PAGE_SIZE),
                             functools.partial(
                                 lambda b, j, pt, i: (layer, pt[b, j * npg + i], 0, 0), i=i))
                for i in range(npg)]
    grid_spec = pltpu.PrefetchScalarGridSpec(
        num_scalar_prefetch=1,
        grid=(bsz, steps),
        in_specs=[
            pl.BlockSpec((1, lq, 512), lambda b, j, pt: (b, 0, B_IQ // 512)),
            pl.BlockSpec((1, lq, LANES), lambda b, j, pt: (b, 0, B_SM // LANES)),
            pl.BlockSpec((1, PAGE_SIZE, LANES), lambda b, j, pt: (b, 0, 0)),
        ] + ik_specs,
        out_specs=[pl.BlockSpec((1, lq, npg * PAGE_SIZE), lambda b, j, pt: (b, 0, j)),
                   pl.BlockSpec((1, lq, PAGE_SIZE), lambda b, j, pt: (b, 0, 0))],
    )
    return pl.pallas_call(
        functools.partial(_sample_score_kernel, npg=npg),
        grid_spec=grid_spec,
        out_shape=[jax.ShapeDtypeStruct((bsz, lq, n_pages * PAGE_SIZE), F32),
                   jax.ShapeDtypeStruct((bsz, lq, PAGE_SIZE), F32)],
        compiler_params=_cparams("parallel", "arbitrary"),
        name="sample_scores",
    )(page_table, yb3, yb3, sm_pad, *([pool_ik_t] * npg))


SEL_CK = 512


def _sample_select_kernel(s_ref, o_ref, key_sc, bias_sc, *, n_top, idx_bits):
    nch = key_sc.shape[0]
    for c in range(nch):
        key_sc[c] = _sortable(s_ref[:, c * SEL_CK:(c + 1) * SEL_CK])
    _select_topk(key_sc, bias_sc, nch, n_top, idx_bits)
    for c in range(nch):
        o_ref[:, c * SEL_CK:(c + 1) * SEL_CK] = bias_sc[c]


def _sample_select(scores, n_top):
    rows, width = scores.shape
    nch = width // SEL_CK
    kernel = functools.partial(_sample_select_kernel, n_top=n_top,
                               idx_bits=max(1, (width - 1).bit_length()))
    return pl.pallas_call(
        kernel,
        grid=(1,),
        in_specs=[pl.BlockSpec((rows, width), lambda i: (0, 0))],
        out_specs=pl.BlockSpec((rows, width), lambda i: (0, 0)),
        out_shape=jax.ShapeDtypeStruct((rows, width), F32),
        scratch_shapes=[pltpu.VMEM((nch, rows, SEL_CK), I32),
                        pltpu.VMEM((nch, rows, SEL_CK), F32)],
        compiler_params=_cparams("arbitrary"),
        name="sample_select",
    )(scores)


def _sample_attn_kernel(pt_ref, q_ref, bias_ref, biasn_ref, kn_ref, vn_ref, *refs,):
    del pt_ref
    npg = PAGES_PER_STEP
    k_refs = refs[:npg]
    v_refs = refs[npg:2 * npg]
    o_ref, m_sc, l_sc, acc_sc = refs[2 * npg:]
    j = pl.program_id(1)
    lq = q_ref.shape[1]

    @pl.when(j == 0)
    def _():
        m_sc[...] = jnp.full(m_sc.shape, NEG_BIG, F32)
        l_sc[...] = jnp.zeros_like(l_sc)
        acc_sc[...] = jnp.zeros_like(acc_sc)

    q = q_ref[0]
    q_all = jnp.concatenate([q[:, h * HEAD_DIM:(h + 1) * HEAD_DIM] for h in range(N_HEADS)],
                            axis=0).astype(BF16)

    def head_rows(ref, g):
        return ref[0, 0, pl.ds(g, PAGE_SIZE, stride=KV_HEADS), :]

    def head_major(page_refs):
        return jnp.concatenate([head_rows(r, g) for g in range(KV_HEADS) for r in page_refs],
                               axis=0).astype(BF16)

    def accumulate(k_pages, v_pages, bias):
        own = jnp.concatenate([bias] * Q_PER_KV, axis=0)
        off = jnp.full(own.shape, NEG_BIG, F32)
        bias_all = jnp.concatenate(
            [jnp.concatenate([own if kh == g else off for kh in range(KV_HEADS)], axis=1)
             for g in range(KV_HEADS)], axis=0)
        s = _dot_nt(q_all, head_major(k_pages)) + bias_all
        m_new, l_new, acc_new = _flash_step(s, head_major(v_pages), m_sc[...], l_sc[...],
                                            acc_sc[...])
        m_sc[...] = m_new
        l_sc[...] = l_new
        acc_sc[...] = acc_new

    accumulate(k_refs, v_refs, bias_ref[0])

    @pl.when(j == pl.num_programs(1) - 1)
    def _():
        accumulate([kn_ref], [vn_ref], biasn_ref[0])
        o = acc_sc[...] / l_sc[...]
        for h in range(N_HEADS):
            o_ref[0, :, h * HEAD_DIM:(h + 1) * HEAD_DIM] = o[h * lq:(h + 1) * lq].astype(o_ref.dtype)


def _sample_attn(page_table, yb3, bias3, kn_pad, vn_pad, pool_k, pool_v, layer):
    bsz, lq, _ = yb3.shape
    n_pages = page_table.shape[1]
    npg = PAGES_PER_STEP
    steps = n_pages // npg
    past = n_pages * PAGE_SIZE
    page_rows = PAGE_SIZE * KV_HEADS

    def page_spec(i):
        return pl.BlockSpec((1, 1, page_rows, HEAD_DIM),
                            functools.partial(
                                lambda b, j, pt, i: (layer, pt[b, j * npg + i], 0, 0), i=i))

    grid_spec = pltpu.PrefetchScalarGridSpec(
        num_scalar_prefetch=1,
        grid=(bsz, steps),
        in_specs=[
            pl.BlockSpec((1, lq, ATT_DIM), lambda b, j, pt: (b, 0, B_Q // ATT_DIM)),
            pl.BlockSpec((1, lq, npg * PAGE_SIZE), lambda b, j, pt: (b, 0, j)),
            pl.BlockSpec((1, lq, PAGE_SIZE), lambda b, j, pt: (b, 0, past // PAGE_SIZE)),
            pl.BlockSpec((1, 1, page_rows, HEAD_DIM), lambda b, j, pt: (b, 0, 0, 0)),
            pl.BlockSpec((1, 1, page_rows, HEAD_DIM), lambda b, j, pt: (b, 0, 0, 0)),
        ] + [page_spec(i) for i in range(npg)] + [page_spec(i) for i in range(npg)],
        out_specs=pl.BlockSpec((1, lq, ATT_DIM), lambda b, j, pt: (b, 0, 0)),
        scratch_shapes=[pltpu.VMEM((N_HEADS * lq, LANES), F32),
                        pltpu.VMEM((N_HEADS * lq, LANES), F32),
                        pltpu.VMEM((N_HEADS * lq, HEAD_DIM), F32)],
    )
    return pl.pallas_call(
        _sample_attn_kernel,
        grid_spec=grid_spec,
        out_shape=jax.ShapeDtypeStruct((bsz, lq, ATT_DIM), BF16),
        compiler_params=_cparams("parallel", "arbitrary"),
        name="sample_attn",
    )(page_table, yb3, bias3, bias3, kn_pad, vn_pad, *([pool_k] * npg), *([pool_v] * npg))


def _ssd_kernel(xs_ref, z_ref, bm_ref, cm_ref, sm_ref, cs_ref, h0_ref, cw_ref, cb_ref,
                dtb_ref, aneg_ref, dsk_ref, nrm_ref, exp_ref,
                y_ref, hout_ref, xp_sc, sm_sc, ht_sc, *, rows, has_state, lead):
    c = pl.program_id(1)
    lc = SSD_CHUNK
    pad = 8
    hist = CONV_W - 1

    def blk(ref):
        return ref[0] if lead else ref[...]

    @pl.when(c == 0)
    def _():
        xp_sc[...] = jnp.zeros_like(xp_sc)
        sm_sc[...] = jnp.zeros_like(sm_sc)
        if has_state:
            xp_sc[pad - hist:pad, :] = cs_ref[0, 0]
            for jp in range(SSD_HEADS // 2):
                pair = jnp.concatenate([h0_ref[0, 0, 2 * jp], h0_ref[0, 0, 2 * jp + 1]], axis=0)
                ht_sc[:, jp * LANES:(jp + 1) * LANES] = pair.T
        else:
            ht_sc[...] = jnp.zeros_like(ht_sc)

    @pl.when(c > 0)
    def _():
        xp_sc[0:pad, :] = xp_sc[lc:lc + pad, :]

    xp_sc[pad:pad + rows, 0:D_INNER] = blk(xs_ref)
    xp_sc[pad:pad + rows, D_INNER:D_INNER + BC_DIM] = blk(bm_ref)
    xp_sc[pad:pad + rows, D_INNER + BC_DIM:CONV_CH] = blk(cm_ref)
    sm_sc[0:rows, :] = blk(sm_ref)

    conv = cb_ref[...]
    for i in range(CONV_W):
        conv = conv + cw_ref[i:i + 1, :] * xp_sc[pad - hist + i:pad - hist + i + lc, :]
    conv = _silu(conv)
    xs = conv[:, 0:D_INNER]
    bmat = conv[:, D_INNER:D_INNER + BC_DIM]
    cmat = conv[:, D_INNER + BC_DIM:CONV_CH]

    lane = lax.broadcasted_iota(I32, (lc, LANES), 1)
    row = lax.broadcasted_iota(I32, (lc, LANES), 0)
    dt_lanes = (lane >= SM_DT) & (lane < SM_DT + SSD_HEADS) & (row < rows)
    raw = sm_sc[...] + dtb_ref[...]
    softplus = jnp.maximum(raw, 0.0) + jnp.log(1.0 + jnp.exp(-jnp.abs(raw)))
    dt = jnp.where(dt_lanes, softplus, 0.0)
    a = dt * aneg_ref[...]
    ltri = (row >= lane).astype(F32)
    acum = jnp.dot(ltri, a, preferred_element_type=F32, precision=lax.Precision.HIGHEST)
    acum_t = acum.T
    dt_t = dt.T
    a_last = acum[lc - 1:lc, :]
    w_end = jnp.exp(a_last - acum) * dt
    expand = exp_ref[...]
    w_exp = _dot(w_end.astype(BF16), expand)
    e_exp = _dot(jnp.exp(acum).astype(BF16), expand)
    cd_exp = jnp.dot(jnp.broadcast_to(jnp.exp(a_last), (8, LANES)), expand.astype(F32),
                     preferred_element_type=F32, precision=lax.Precision.HIGHEST)[0:1, :]
    xw = (xs * w_exp).astype(BF16)
    xs_b = xs.astype(BF16)
    causal = row >= lane

    ht_prev = ht_sc[...]
    gw = D_INNER // SSD_GROUPS
    heads_pg = SSD_HEADS // SSD_GROUPS
    y_parts = []
    for g in range(SSD_GROUPS):
        bg = bmat[:, g * D_STATE:(g + 1) * D_STATE]
        cg = cmat[:, g * D_STATE:(g + 1) * D_STATE].astype(BF16)
        cb = _dot_nt(cg, bg.astype(BF16))
        y_off = _dot(cg, ht_prev[:, g * gw:(g + 1) * gw].astype(BF16)) * e_exp[:, g * gw:(g + 1) * gw]
        states_t = _dot(bg.T.astype(BF16), xw[:, g * gw:(g + 1) * gw])
        ht_sc[:, g * gw:(g + 1) * gw] = (ht_prev[:, g * gw:(g + 1) * gw] * cd_exp[:, g * gw:(g + 1) * gw]
                                         + states_t)
        diag = []
        for jp in range(heads_pg // 2):
            ms = []
            for t in range(2):
                h = g * heads_pg + 2 * jp + t
                col = acum[:, SM_DT + h:SM_DT + h + 1]
                rowv = acum_t[SM_DT + h:SM_DT + h + 1, :]
                dec = jnp.exp(jnp.where(causal, col - rowv, NEG_BIG))
                ms.append((cb * dec * dt_t[SM_DT + h:SM_DT + h + 1, :]).astype(BF16))
            mpair = jnp.concatenate(ms, axis=1)
            xpair = xs_b[:, g * gw + jp * LANES:g * gw + (jp + 1) * LANES]
            zero = jnp.zeros_like(xpair)
            xbd = jnp.concatenate([jnp.where(lane < SSD_HEAD_DIM, xpair, zero),
                                   jnp.where(lane >= SSD_HEAD_DIM, xpair, zero)], axis=0)
            diag.append(_dot(mpair, xbd))
        y_parts.append(jnp.concatenate(diag, axis=1) + y_off)
    y = jnp.concatenate(y_parts, axis=1) + dsk_ref[...] * xs

    zval = blk(z_ref)
    if rows < lc:
        zval = jnp.concatenate([zval, jnp.zeros((lc - rows, D_INNER), F32)], axis=0)
    yz = y * _silu(zval)
    outs = []
    for g in range(SSD_GROUPS):
        part = yz[:, g * gw:(g + 1) * gw]
        part = part * lax.rsqrt(jnp.mean(part * part, axis=-1, keepdims=True) + EPS)
        outs.append(part * nrm_ref[:, g * gw:(g + 1) * gw])
    out = jnp.concatenate(outs, axis=1)
    if lead:
        y_ref[0] = out[0:rows].astype(y_ref.dtype)
    else:
        y_ref[...] = out.astype(y_ref.dtype)

    @pl.when(c == pl.num_programs(1) - 1)
    def _():
        for jp in range(SSD_HEADS // 2):
            pair = ht_sc[:, jp * LANES:(jp + 1) * LANES].T
            hout_ref[0, 2 * jp] = pair[0:SSD_HEAD_DIM]
            hout_ref[0, 2 * jp + 1] = pair[SSD_HEAD_DIM:]


def _ssd_consts(p, l):
    pad_l = SM_DT
    pad_r = LANES - SM_DT - SSD_HEADS

    def lanes(v):
        return jnp.pad(v.astype(F32), (pad_l, pad_r)).reshape(1, LANES)

    a_neg = -jnp.exp(p['a_log'][l].astype(F32))
    heads = jnp.arange(D_INNER, dtype=I32) // SSD_HEAD_DIM
    expand = (jnp.arange(LANES, dtype=I32)[:, None] == heads[None, :] + SM_DT).astype(BF16)
    return dict(
        cw=p['conv_w'][l], cb=p['conv_b'][l].reshape(1, CONV_CH),
        dtb=lanes(p['dt_bias'][l]), aneg=lanes(a_neg),
        dsk=jnp.repeat(p['d_skip'][l].astype(F32), SSD_HEAD_DIM).reshape(1, D_INNER),
        nrm=p['norm_ssd'][l].reshape(1, D_INNER), expand=expand)


def _ssd_prompt(ya, yb, bsz, seq, k):
    nc = seq // SSD_CHUNK
    lc = SSD_CHUNK
    const = lambda shape: pl.BlockSpec(shape, lambda b, c: (0,) * len(shape))
    dummy_cs = jnp.zeros((1, 1, CONV_W - 1, CONV_CH), F32)
    dummy_h0 = jnp.zeros((1, 1, 2, SSD_HEAD_DIM, D_STATE), F32)
    return pl.pallas_call(
        functools.partial(_ssd_kernel, rows=lc, has_state=False, lead=False),
        grid=(bsz, nc),
        in_specs=[
            pl.BlockSpec((lc, D_INNER), lambda b, c: (b * nc + c, A_XS // D_INNER)),
            pl.BlockSpec((lc, D_INNER), lambda b, c: (b * nc + c, A_Z // D_INNER)),
            pl.BlockSpec((lc, BC_DIM), lambda b, c: (b * nc + c, A_BM // BC_DIM)),
            pl.BlockSpec((lc, BC_DIM), lambda b, c: (b * nc + c, A_CM // BC_DIM)),
            pl.BlockSpec((lc, LANES), lambda b, c: (b * nc + c, B_SM // LANES)),
            const((1, 1, CONV_W - 1, CONV_CH)),
            const((1, 1, 2, SSD_HEAD_DIM, D_STATE)),
            const((CONV_W, CONV_CH)), const((1, CONV_CH)),
            const((1, LANES)), const((1, LANES)),
            const((1, D_INNER)), const((1, D_INNER)), const((LANES, D_INNER)),
        ],
        out_specs=[pl.BlockSpec((lc, D_INNER), lambda b, c: (b * nc + c, 0)),
                   pl.BlockSpec((1, SSD_HEADS, SSD_HEAD_DIM, D_STATE), lambda b, c: (b, 0, 0, 0))],
        out_shape=[jax.ShapeDtypeStruct((bsz * seq, D_INNER), BF16),
                   jax.ShapeDtypeStruct((bsz, SSD_HEADS, SSD_HEAD_DIM, D_STATE), F32)],
        scratch_shapes=[pltpu.VMEM((lc + 8, CONV_CH), F32), pltpu.VMEM((lc, LANES), F32),
                        pltpu.VMEM((D_STATE, D_INNER), F32)],
        compiler_params=_cparams("parallel", "arbitrary"),
        name="ssd_prompt",
    )(ya, ya, ya, ya, yb, dummy_cs, dummy_h0, k['cw'], k['cb'], k['dtb'], k['aneg'],
      k['dsk'], k['nrm'], k['expand'])


def _ssd_sample(ya3, yb3, conv_state, ssm_state, k, layer):
    bsz, lq, _ = ya3.shape
    lc = SSD_CHUNK
    const = lambda shape: pl.BlockSpec(shape, lambda b, c: (0,) * len(shape))
    return pl.pallas_call(
        functools.partial(_ssd_kernel, rows=lq, has_state=True, lead=True),
        grid=(bsz, 1),
        in_specs=[
            pl.BlockSpec((1, lq, D_INNER), lambda b, c: (b, 0, A_XS // D_INNER)),
            pl.BlockSpec((1, lq, D_INNER), lambda b, c: (b, 0, A_Z // D_INNER)),
            pl.BlockSpec((1, lq, BC_DIM), lambda b, c: (b, 0, A_BM // BC_DIM)),
            pl.BlockSpec((1, lq, BC_DIM), lambda b, c: (b, 0, A_CM // BC_DIM)),
            pl.BlockSpec((1, lq, LANES), lambda b, c: (b, 0, B_SM // LANES)),
            pl.BlockSpec((1, 1, CONV_W - 1, CONV_CH), lambda b, c: (layer, b, 0, 0)),
            pl.BlockSpec((1, 1, SSD_HEADS, SSD_HEAD_DIM, D_STATE),
                         lambda b, c: (layer, b, 0, 0, 0)),
            const((CONV_W, CONV_CH)), const((1, CONV_CH)),
            const((1, LANES)), const((1, LANES)),
            const((1, D_INNER)), const((1, D_INNER)), const((LANES, D_INNER)),
        ],
        out_specs=[pl.BlockSpec((1, lq, D_INNER), lambda b, c: (b, 0, 0)),
                   pl.BlockSpec((1, SSD_HEADS, SSD_HEAD_DIM, D_STATE), lambda b, c: (b, 0, 0, 0))],
        out_shape=[jax.ShapeDtypeStruct((bsz, lq, D_INNER), BF16),
                   jax.ShapeDtypeStruct((bsz, SSD_HEADS, SSD_HEAD_DIM, D_STATE), F32)],
        scratch_shapes=[pltpu.VMEM((lc + 8, CONV_CH), F32), pltpu.VMEM((lc, LANES), F32),
                        pltpu.VMEM((D_STATE, D_INNER), F32)],
        compiler_params=_cparams("parallel", "arbitrary"),
        name="ssd_sample",
    )(ya3, ya3, ya3, ya3, yb3, conv_state, ssm_state, k['cw'], k['cb'], k['dtb'], k['aneg'],
      k['dsk'], k['nrm'], k['expand'])


def _merge_kernel(x_ref, gt_ref, at_ref, sd_ref, ga_ref, gb_ref, wa_ref, wb_ref, wo_ref, o_ref):
    ma = _dot(at_ref[...], wa_ref[...])
    mb = _dot(sd_ref[...], wb_ref[...])
    merged = jax.nn.sigmoid(ga_ref[...]) * ma + jax.nn.sigmoid(gb_ref[...]) * mb
    o_ref[...] = x_ref[...] + gt_ref[0] * _dot(merged.astype(BF16), wo_ref[...])


def _merge(x, gt, attn, ssd, ya, wa, wb, wo, *, tm):
    t, d = x.shape
    groups, r, _ = gt.shape
    tpg = (t // groups) // tm
    return pl.pallas_call(
        _merge_kernel,
        grid=(t // tm,),
        in_specs=[
            pl.BlockSpec((tm, d), lambda i: (i, 0)),
            pl.BlockSpec((1, r, d), lambda i: (i // tpg, 0, 0)),
            pl.BlockSpec((tm, ATT_DIM), lambda i: (i, 0)),
            pl.BlockSpec((tm, D_INNER), lambda i: (i, 0)),
            pl.BlockSpec((tm, d), lambda i: (i, A_GA // D_MODEL)),
            pl.BlockSpec((tm, d), lambda i: (i, A_GB // D_MODEL)),
            _resident((ATT_DIM, d)), _resident((D_INNER, d)), _resident((d, d)),
        ],
        out_specs=pl.BlockSpec((tm, d), lambda i: (i, 0)),
        out_shape=jax.ShapeDtypeStruct((t, d), F32),
        compiler_params=_cparams("parallel"),
        name="merge",
    )(x, gt, attn, ssd, ya, ya, wa, wb, wo)


def _layer_weights(p, l):
    w_in = p['w_in'][l]
    offs = {}
    o = 0
    for name, size in (('q', ATT_DIM), ('k', KV_DIM), ('v', KV_DIM), ('iq', IDX_HEADS * IDX_DIM),
                       ('ik', IDX_DIM), ('iw', IDX_HEADS), ('z', D_INNER), ('xbc', CONV_CH),
                       ('dt', SSD_HEADS), ('ga', D_MODEL), ('gb', D_MODEL)):
        offs[name] = (o, o + size)
        o += size

    def seg(name):
        s, e = offs[name]
        return w_in[:, s:e]

    xbc = seg('xbc')
    w_a = jnp.concatenate([xbc[:, :D_INNER], seg('z'), seg('ga'), seg('gb'),
                           xbc[:, D_INNER:D_INNER + BC_DIM], xbc[:, D_INNER + BC_DIM:]], axis=1)
    small = jnp.concatenate([seg('ik'), seg('iw'), seg('dt')], axis=1)
    small = jnp.pad(small, ((0, 0), (0, B_DIM - B_SM - small.shape[1])))
    w_b = jnp.concatenate([seg('q'), seg('k'), seg('v'), seg('iq'), small], axis=1)
    return dict(
        w_a=w_a.astype(BF16), w_b=w_b.astype(BF16),
        up1=p['w_ffn1_up'][l].astype(BF16), down1=p['w_ffn1_down'][l].astype(BF16),
        up2=p['w_ffn2_up'][l].astype(BF16), down2=p['w_ffn2_down'][l].astype(BF16),
        wa=p['w_br_a'][l].astype(BF16), wb=p['w_br_b'][l].astype(BF16), wo=p['w_out'][l].astype(BF16),
        g1=p['norm_ffn1'][l].reshape(1, D_MODEL), gm=p['norm_mix'][l].reshape(1, D_MODEL),
        g2=p['norm_ffn2'][l].reshape(1, D_MODEL))


TM_FFN, TM_PROJ_A, TM_PROJ_B, TM_MERGE = 512, 256, 512, 256


def _pre_mix(x, mods, w, tables, rows):
    sh1, sc1, g1, sh2, sc2 = mods[0], mods[1], mods[2], mods[3], mods[4]
    x = _ffn(x, sh1, sc1, g1, w['g1'], w['g1'], w['up1'], w['down1'],
             tm=min(TM_FFN, rows), final_norm=False)
    ya = _proj_a(x, sh2, sc2, w['gm'], w['w_a'], tm=min(TM_PROJ_A, rows))
    yb, ybh = _proj_b(x, sh2, sc2, w['gm'], w['w_b'], tables[0], tables[1],
                      tm=min(TM_PROJ_B, rows))
    return x, ya, yb, ybh


def _post_mix(x, mods, w, attn, ssd, ya, gf, rows, final_norm):
    x = _merge(x, mods[5], attn, ssd, ya, w['wa'], w['wb'], w['wo'], tm=min(TM_MERGE, rows))
    return _ffn(x, mods[6], mods[7], mods[8], w['g2'], gf, w['up2'], w['down2'],
                tm=min(TM_FFN, rows), final_norm=final_norm)


def kernel(x_prompt, x_sample, cache_k, cache_v, cache_idx_k, state_ssm, state_conv, page_table,
           c_prompt, c_sample, w_ada, b_ada, norm_ffn1, w_ffn1_up, w_ffn1_down, norm_mix, w_in,
           conv_w, conv_b, dt_bias, a_log, d_skip, norm_ssd, w_br_a, w_br_b, w_out,
           norm_ffn2, w_ffn2_up, w_ffn2_down, norm_final):
    p = {'w_ada': w_ada, 'b_ada': b_ada, 'norm_ffn1': norm_ffn1, 'w_ffn1_up': w_ffn1_up,
         'w_ffn1_down': w_ffn1_down, 'norm_mix': norm_mix, 'w_in': w_in, 'conv_w': conv_w,
         'conv_b': conv_b, 'dt_bias': dt_bias, 'a_log': a_log, 'd_skip': d_skip,
         'norm_ssd': norm_ssd, 'w_br_a': w_br_a, 'w_br_b': w_br_b, 'w_out': w_out,
         'norm_ffn2': norm_ffn2, 'w_ffn2_up': w_ffn2_up, 'w_ffn2_down': w_ffn2_down}
    depth = w_in.shape[0]
    bsz, seq, d = x_prompt.shape
    dbsz, dseq, _ = x_sample.shape
    n_pages = page_table.shape[1]
    past_len = n_pages * PAGE_SIZE
    n_pool = cache_k.shape[1]
    tp = bsz * seq
    ts = dbsz * dseq

    rows = bsz + dbsz
    rpad = -rows % 8
    c_all = jnp.pad(jnp.concatenate([c_prompt, c_sample], axis=0), ((0, rpad), (0, 0)))
    mods_all = _ada(c_all, w_ada, b_ada)

    def mods_for(l):
        mp, ms = [], []
        for i in range(N_MODS):
            m = mods_all[l, :, i * d:(i + 1) * d]
            mp.append(m[:bsz].reshape(bsz, 1, d))
            ms.append(jnp.repeat(m[bsz:rows], dseq, axis=0).reshape(1, ts, d))
        return mp, ms

    tab_p = _rope_tables(jnp.arange(seq, dtype=I32))
    tab_s = _rope_tables(past_len + jnp.tile(jnp.arange(dseq, dtype=I32), dbsz))
    gf = norm_final.reshape(1, d)
    n_top_p = min(INDEX_TOPK, seq // 4)
    n_top_s = min(INDEX_TOPK, (past_len + dseq) // 4)

    pool_k = cache_k.reshape(depth, n_pool, PAGE_SIZE * KV_HEADS, HEAD_DIM)
    pool_v = cache_v.reshape(depth, n_pool, PAGE_SIZE * KV_HEADS, HEAD_DIM)
    pool_ik_t = jnp.swapaxes(cache_idx_k, 2, 3)

    xp = x_prompt.reshape(tp, d)
    xs = x_sample.reshape(ts, d)
    outs = {k: [] for k in ('kp', 'vp', 'ikp', 'sp', 'cp', 'ks', 'vs', 'iks', 'ss', 'cs')}
    for l in range(depth):
        w = _layer_weights(p, l)
        kc = _ssd_consts(p, l)
        mp, ms = mods_for(l)
        last = l == depth - 1

        xp, ya, yb, ybh = _pre_mix(xp, mp, w, tab_p, seq)
        attn = _dsa_prompt(yb, ybh, bsz, seq, n_top_p)
        ssd, h_fin = _ssd_prompt(ya, yb, bsz, seq, kc)
        xp = _post_mix(xp, mp, w, attn, ssd, ya, gf, seq, last)
        yb4 = yb.reshape(bsz, seq, B_DIM)
        ya4 = ya.reshape(bsz, seq, A_DIM)
        outs['kp'].append(yb4[:, :, B_K:B_K + KV_DIM].reshape(bsz, seq, KV_HEADS, HEAD_DIM))
        outs['vp'].append(yb4[:, :, B_V:B_V + KV_DIM].reshape(bsz, seq, KV_HEADS, HEAD_DIM))
        outs['ikp'].append(yb4[:, :, B_SM:B_SM + IDX_DIM])
        outs['sp'].append(h_fin)
        tail = ya4[:, seq - (CONV_W - 1):, :]
        outs['cp'].append(jnp.concatenate(
            [tail[:, :, A_XS:A_XS + D_INNER], tail[:, :, A_BM:A_BM + BC_DIM],
             tail[:, :, A_CM:A_CM + BC_DIM]], axis=-1))

        xs, ya, yb, _ = _pre_mix(xs, ms, w, tab_s, ts)
        ya3 = ya.reshape(dbsz, dseq, A_DIM)
        yb3 = yb.reshape(dbsz, dseq, B_DIM)
        rpad_s = PAGE_SIZE - dseq
        sm_pad = jnp.pad(yb3[:, :, B_SM:B_SM + LANES], ((0, 0), (0, rpad_s), (0, 0)))
        page_rows = PAGE_SIZE * KV_HEADS
        kn_pad = jnp.pad(yb3[:, :, B_K:B_K + KV_DIM], ((0, 0), (0, rpad_s), (0, 0)))
        vn_pad = jnp.pad(yb3[:, :, B_V:B_V + KV_DIM], ((0, 0), (0, rpad_s), (0, 0)))
        kn_pad = kn_pad.reshape(dbsz, 1, page_rows, HEAD_DIM)
        vn_pad = vn_pad.reshape(dbsz, 1, page_rows, HEAD_DIM)
        s_past, s_new = _sample_scores(page_table, yb3, sm_pad, pool_ik_t, l)
        s_fill = jnp.full((dbsz, dseq, SEL_CK - PAGE_SIZE), -jnp.inf, F32)
        scores = jnp.concatenate([s_past, s_new, s_fill], axis=-1).reshape(ts, past_len + SEL_CK)
        bias = _sample_select(scores, n_top_s).reshape(dbsz, dseq, past_len + SEL_CK)
        attn = _sample_attn(page_table, yb3, bias, kn_pad, vn_pad, pool_k, pool_v, l)
        ssd, h_fin = _ssd_sample(ya3, yb3, state_conv, state_ssm, kc, l)
        xs = _post_mix(xs, ms, w, attn.reshape(ts, ATT_DIM), ssd.reshape(ts, D_INNER), ya, gf,
                       ts, last)
        outs['ks'].append(yb3[:, :, B_K:B_K + KV_DIM].reshape(dbsz, dseq, KV_HEADS, HEAD_DIM))
        outs['vs'].append(yb3[:, :, B_V:B_V + KV_DIM].reshape(dbsz, dseq, KV_HEADS, HEAD_DIM))
        outs['iks'].append(yb3[:, :, B_SM:B_SM + IDX_DIM])
        outs['ss'].append(h_fin)
        xbc = jnp.concatenate([ya3[:, :, A_XS:A_XS + D_INNER], ya3[:, :, A_BM:A_BM + BC_DIM],
                               ya3[:, :, A_CM:A_CM + BC_DIM]], axis=-1)
        xpad = jnp.concatenate([state_conv[l], xbc], axis=1)
        outs['cs'].append(xpad[:, dseq:])

    return (xp.reshape(bsz, seq, d), xs.reshape(dbsz, dseq, d),
            jnp.stack(outs['kp']), jnp.stack(outs['vp']), jnp.stack(outs['ikp']),
            jnp.stack(outs['sp']), jnp.stack(outs['cp']),
            jnp.stack(outs['ks']), jnp.stack(outs['vs']), jnp.stack(outs['iks']),
            jnp.stack(outs['ss']), jnp.stack(outs['cs']))
```

```python
import functools
import math

import jax
import jax.numpy as jnp
from jax import lax
from jax.experimental import pallas as pl
from jax.experimental.pallas import tpu as pltpu

F32 = jnp.float32
BF16 = jnp.bfloat16
I32 = jnp.int32

D_MODEL = 1024
PAGE_SIZE = 128
N_HEADS = 8
HEAD_DIM = 128
KV_HEADS = 4
Q_PER_KV = N_HEADS // KV_HEADS
ATT_DIM = N_HEADS * HEAD_DIM
KV_DIM = KV_HEADS * HEAD_DIM
ATT_SCALE = HEAD_DIM ** -0.5
IDX_HEADS = 8
IDX_DIM = 64
IDX_ROPE_DIM = 32
IDX_SCALE = IDX_HEADS ** -0.5 * IDX_DIM ** -0.5
INDEX_TOPK = 256
ROPE_THETA = 10000.0
D_INNER = 2 * D_MODEL
SSD_HEAD_DIM = 64
SSD_HEADS = D_INNER // SSD_HEAD_DIM
SSD_GROUPS = 4
D_STATE = 128
CONV_W = 4
BC_DIM = SSD_GROUPS * D_STATE
CONV_CH = D_INNER + 2 * BC_DIM
SSD_CHUNK = 128
FFN_DIM = 2816
N_MODS = 9
EPS = 1e-6

LANES = 128
NEG_BIG = -1e30
INT_MIN = -(2 ** 31)
NEG_INF_KEY = -2139095041
VMEM_LIMIT = 56 * 1024 * 1024

A_XS, A_Z, A_GA, A_GB, A_BM, A_CM = 0, 2048, 4096, 5120, 6144, 6656
A_DIM = 7168
B_Q, B_K, B_V, B_IQ, B_SM = 0, 1024, 1536, 2048, 2560
B_DIM = 3072
SM_IW = IDX_DIM
SM_DT = IDX_DIM + IDX_HEADS
PROJ_TN = 512


def _cparams(*sem):
    return pltpu.CompilerParams(dimension_semantics=sem, vmem_limit_bytes=VMEM_LIMIT)


def _silu(v):
    return v * jax.nn.sigmoid(v)


def _norm_mod(x, g, shift, scale):
    xn = x * lax.rsqrt(jnp.mean(x * x, axis=-1, keepdims=True) + EPS) * g
    return xn * (1.0 + scale) + shift


def _dot(a, b):
    return jnp.dot(a, b, preferred_element_type=F32)


def _dot_nt(a, b):
    return lax.dot_general(a, b, (((1,), (1,)), ((), ())), preferred_element_type=F32)


def _ada_kernel(c_ref, w_ref, b_ref, o_ref):
    s = _silu(c_ref[...]).astype(BF16)
    o_ref[0] = _dot(s, w_ref[0].astype(BF16)) + b_ref[0]


def _ada(c_all, w_ada, b_ada):
    depth, d, n = w_ada.shape
    r = c_all.shape[0]
    tn = 1024
    return pl.pallas_call(
        _ada_kernel,
        grid=(depth, n // tn),
        in_specs=[
            pl.BlockSpec((r, d), lambda l, j: (0, 0)),
            pl.BlockSpec((1, d, tn), lambda l, j: (l, 0, j)),
            pl.BlockSpec((1, 1, tn), lambda l, j: (l, 0, j)),
        ],
        out_specs=pl.BlockSpec((1, r, tn), lambda l, j: (l, 0, j)),
        out_shape=jax.ShapeDtypeStruct((depth, r, n), F32),
        compiler_params=_cparams("parallel", "parallel"),
        name="adaln",
    )(c_all, w_ada, b_ada.reshape(depth, 1, n))


def _resident(shape):
    return pl.BlockSpec(shape, lambda i: (0,) * len(shape), pipeline_mode=pl.Buffered(1))


def _mod_spec(mod, t, tm):
    groups, r, d = mod.shape
    tpg = (t // groups) // tm
    return pl.BlockSpec((1, r, d), lambda i: (i // tpg, 0, 0))


FFN_TF = 2816


def _ffn_kernel(x_ref, sh_ref, sc_ref, gt_ref, g_ref, gf_ref, wu_ref, wd_ref, o_ref, *, final_norm):
    x = x_ref[...]
    h = _norm_mod(x, g_ref[...], sh_ref[0], sc_ref[0]).astype(BF16)
    f = wd_ref.shape[0]
    acc = None
    for j in range(f // FFN_TF):
        a = _dot(h, wu_ref[:, j * FFN_TF:(j + 1) * FFN_TF])
        b = _dot(h, wu_ref[:, f + j * FFN_TF:f + (j + 1) * FFN_TF])
        t = (_silu(a) * b).astype(BF16)
        part = _dot(t, wd_ref[j * FFN_TF:(j + 1) * FFN_TF, :])
        acc = part if acc is None else acc + part
    y = x + 0.5 * gt_ref[0] * acc
    if final_norm:
        y = y * lax.rsqrt(jnp.mean(y * y, axis=-1, keepdims=True) + EPS) * gf_ref[...]
    o_ref[...] = y


def _ffn(x, sh, sc, gt, g, gf, w_up, w_down, *, tm, final_norm):
    t, d = x.shape
    mod = _mod_spec(sh, t, tm)
    return pl.pallas_call(
        functools.partial(_ffn_kernel, final_norm=final_norm),
        grid=(t // tm,),
        in_specs=[
            pl.BlockSpec((tm, d), lambda i: (i, 0)),
            mod, mod, mod,
            _resident((1, d)), _resident((1, d)),
            _resident(w_up.shape), _resident(w_down.shape),
        ],
        out_specs=pl.BlockSpec((tm, d), lambda i: (i, 0)),
        out_shape=jax.ShapeDtypeStruct((t, d), F32),
        compiler_params=_cparams("parallel"),
        name="ffn",
    )(x, sh, sc, gt, g, gf, w_up, w_down)


def _proj_a_kernel(x_ref, sh_ref, sc_ref, g_ref, w_ref, o_ref):
    h = _norm_mod(x_ref[...], g_ref[...], sh_ref[0], sc_ref[0]).astype(BF16)
    for j in range(w_ref.shape[1] // PROJ_TN):
        cols = slice(j * PROJ_TN, (j + 1) * PROJ_TN)
        o_ref[:, cols] = _dot(h, w_ref[:, cols])


def _proj_b_kernel(x_ref, sh_ref, sc_ref, g_ref, w_ref, t64_ref, t16_ref,
                   o_ref, ob_ref, ok_ref, ov_ref):
    h = _norm_mod(x_ref[...], g_ref[...], sh_ref[0], sc_ref[0]).astype(BF16)
    half = IDX_ROPE_DIM // 2
    tm = x_ref.shape[0]

    def emit(col, v):
        o_ref[:, col:col + LANES] = v
        ob_ref[:, col:col + LANES] = v.astype(BF16)
        if B_K <= col < B_V:
            ok_ref[pl.ds((col - B_K) // HEAD_DIM, tm, stride=KV_HEADS), :] = v
        elif B_V <= col < B_IQ:
            ov_ref[pl.ds((col - B_V) // HEAD_DIM, tm, stride=KV_HEADS), :] = v

    for j in range(w_ref.shape[1] // PROJ_TN):
        base = j * PROJ_TN
        y = _dot(h, w_ref[:, base:base + PROJ_TN])
        for g in range(PROJ_TN // LANES):
            col = base + g * LANES
            yg = y[:, g * LANES:(g + 1) * LANES]
            if col < B_V:
                tab = 0 if col < B_K else 1
                c = t64_ref[tab, :, 0:LANES]
                a = t64_ref[tab, :, LANES:2 * LANES]
                emit(col, yg * c + pltpu.roll(yg, HEAD_DIM // 2, 1) * a)
            elif col < B_IQ:
                emit(col, yg)
            else:
                tab = 0 if col < B_SM else 1
                c = t16_ref[tab, :, 0:LANES]
                a = t16_ref[tab, :, LANES:2 * LANES]
                b = t16_ref[tab, :, 2 * LANES:3 * LANES]
                emit(col, yg * c + pltpu.roll(yg, half, 1) * a
                     + pltpu.roll(yg, LANES - half, 1) * b)


def _proj_a(x, sh, sc, g, w, *, tm):
    t, d = x.shape
    n = w.shape[1]
    mod = _mod_spec(sh, t, tm)
    return pl.pallas_call(
        _proj_a_kernel,
        grid=(t // tm,),
        in_specs=[
            pl.BlockSpec((tm, d), lambda i: (i, 0)),
            mod, mod,
            _resident((1, d)), _resident(w.shape),
        ],
        out_specs=pl.BlockSpec((tm, n), lambda i: (i, 0)),
        out_shape=jax.ShapeDtypeStruct((t, n), F32),
        compiler_params=_cparams("parallel"),
        name="proj_a",
    )(x, sh, sc, g, w)


def _proj_b(x, sh, sc, g, w, t64, t16, *, tm):
    t, d = x.shape
    n = w.shape[1]
    mod = _mod_spec(sh, t, tm)
    npos = t64.shape[1] // tm
    return pl.pallas_call(
        _proj_b_kernel,
        grid=(t // tm,),
        in_specs=[
            pl.BlockSpec((tm, d), lambda i: (i, 0)),
            mod, mod,
            _resident((1, d)), _resident(w.shape),
            pl.BlockSpec((2, tm, 2 * LANES), lambda i: (0, i % npos, 0)),
            pl.BlockSpec((2, tm, 3 * LANES), lambda i: (0, i % npos, 0)),
        ],
        out_specs=[pl.BlockSpec((tm, n), lambda i: (i, 0)),
                   pl.BlockSpec((tm, n), lambda i: (i, 0)),
                   pl.BlockSpec((tm * KV_HEADS, HEAD_DIM), lambda i: (i, 0)),
                   pl.BlockSpec((tm * KV_HEADS, HEAD_DIM), lambda i: (i, 0))],
        out_shape=[jax.ShapeDtypeStruct((t, n), F32), jax.ShapeDtypeStruct((t, n), BF16),
                   jax.ShapeDtypeStruct((t * KV_HEADS, HEAD_DIM), F32),
                   jax.ShapeDtypeStruct((t * KV_HEADS, HEAD_DIM), F32)],
        compiler_params=_cparams("parallel"),
        name="proj_b",
    )(x, sh, sc, g, w, t64, t16)


def _rope_tables(pos):
    posf = pos.astype(F32)[:, None]
    half = HEAD_DIM // 2
    inv = ROPE_THETA ** (-jnp.arange(half, dtype=F32) / half)
    ang = posf * inv[None, :]
    cos, sin = jnp.cos(ang), jnp.sin(ang)
    t_k = jnp.concatenate([cos, cos, -sin, sin], axis=1)
    t64 = jnp.stack([t_k * ATT_SCALE, t_k])
    ih = IDX_ROPE_DIM // 2
    inv_i = ROPE_THETA ** (-jnp.arange(ih, dtype=F32) / ih)
    ang_i = posf * inv_i[None, :]
    ci, si = jnp.cos(ang_i), jnp.sin(ang_i)
    n = pos.shape[0]
    ones = jnp.ones((n, IDX_DIM - IDX_ROPE_DIM), F32)
    zeros = jnp.zeros((n, IDX_DIM - IDX_ROPE_DIM), F32)
    zh = jnp.zeros((n, ih), F32)
    c_head = jnp.concatenate([ci, ci, ones], axis=1)
    a_head = jnp.concatenate([zh, si, zeros], axis=1)
    b_head = jnp.concatenate([-si, zh, zeros], axis=1)
    t_iq = jnp.concatenate([c_head, c_head, a_head, a_head, b_head, b_head], axis=1)
    z64 = jnp.zeros((n, IDX_DIM), F32)
    c_tail = jnp.concatenate([jnp.full((n, IDX_HEADS), IDX_SCALE, F32),
                              jnp.ones((n, LANES - IDX_DIM - IDX_HEADS), F32)], axis=1)
    t_sm = jnp.concatenate([c_head, c_tail, a_head, z64, b_head, z64], axis=1)
    return t64, jnp.stack([t_iq, t_sm])


def _sortable(x):
    b = lax.bitcast_convert_type(x, I32)
    b = jnp.where(b == INT_MIN, 0, b)
    return jnp.where(b < 0, b ^ 0x7FFFFFFF, b)


def _select_topk(key_sc, bias_sc, nk, n_top, idx_bits):
    _, rows, ck = key_sc.shape
    nblk = ck // LANES
    rb = min(rows, 128)

    def col_index(nrows, c, t):
        return lax.broadcasted_iota(I32, (nrows, LANES), 1) + (c * ck + t * LANES)

    def lanes(rep, rs):
        return rep[rs]

    def row_sum(acc):
        return jnp.broadcast_to(jnp.sum(acc, axis=1, keepdims=True), acc.shape)

    def count(make_pred):
        slices = [slice(r * rb, (r + 1) * rb) for r in range(rows // rb)]
        preds = [make_pred(rs) for rs in slices]
        accs = []
        for rs, pred in zip(slices, preds):
            def body(c, acc, rs=rs, pred=pred):
                for t in range(nblk):
                    k = key_sc[c, rs, t * LANES:(t + 1) * LANES]
                    acc = acc + jnp.where(pred(k, c, t), 1.0, 0.0)
                return acc
            accs.append(lax.fori_loop(0, nk, body, jnp.zeros((rb, LANES), F32)))
        acc = accs[0] if len(accs) == 1 else jnp.concatenate(accs, axis=0)
        return row_sum(acc)

    def ge(col):
        def make(rs):
            cb = lanes(col, rs)
            return lambda k, c, t: k >= cb
        return make

    def gt(col):
        def make(rs):
            cb = lanes(col, rs)
            return lambda k, c, t: k > cb
        return make

    k_top = float(n_top)

    def vstep(it, lo):
        cand = lo + lax.shift_left(jnp.int32(1), 31 - it)
        cnt = count(ge(cand))
        return jnp.where(cnt >= k_top, cand, lo)

    lo = lax.fori_loop(0, 32, vstep, jnp.full((rows, LANES), INT_MIN, I32))
    cnt_gt = count(gt(lo))
    cnt_ge = count(ge(lo))
    need = k_top - cnt_gt
    excess = (cnt_ge > k_top) & (lo > NEG_INF_KEY)
    any_excess = jnp.max(excess.astype(I32)) > 0

    @pl.when(jnp.logical_not(any_excess))
    def _():
        thr = jnp.maximum(lo, NEG_INF_KEY + 1)

        def wbody(c, carry):
            for t in range(nblk):
                sl = slice(t * LANES, (t + 1) * LANES)
                bias_sc[c, :, sl] = jnp.where(key_sc[c, :, sl] >= thr, 0.0, NEG_BIG)
            return carry

        lax.fori_loop(0, nk, wbody, 0)

    @pl.when(any_excess)
    def _():
        def tie_below(cand):
            def make(rs):
                lob, cb = lanes(lo, rs), lanes(cand, rs)
                return lambda k, c, t: (k == lob) & (col_index(rb, c, t) < cb)
            return make

        def jstep(it, m):
            cand = m + lax.shift_left(jnp.int32(1), idx_bits - 1 - it)
            f = count(tie_below(cand))
            return jnp.where(f < need, cand, m)
        m = lax.fori_loop(0, idx_bits, jstep, jnp.zeros((rows, LANES), I32))
        jmax = jnp.where(excess, m, jnp.int32(2 ** 30))

        def wbody(c, carry):
            for t in range(nblk):
                sl = slice(t * LANES, (t + 1) * LANES)
                k = key_sc[c, :, sl]
                sel = (k > lo) | ((k == lo) & (col_index(rows, c, t) <= jmax))
                sel = sel & (k > NEG_INF_KEY)
                bias_sc[c, :, sl] = jnp.where(sel, 0.0, NEG_BIG)
            return carry

        lax.fori_loop(0, nk, wbody, 0)


def _select_topk_cols(key_sc, bias_sc, nk, n_top, idx_bits):
    _, ck, nq = key_sc.shape
    sub = 8
    nslab = ck // sub
    nacc = 4
    k_top = float(n_top)

    def key_index(c, r):
        return lax.broadcasted_iota(I32, (sub, nq), 0) + (c * ck + r * sub)

    def count(pred):
        def body(c, accs):
            accs = list(accs)
            for r in range(nslab):
                k = key_sc[c, r * sub:(r + 1) * sub, :]
                accs[r % nacc] = accs[r % nacc] + jnp.where(pred(k, c, r), 1.0, 0.0)
            return tuple(accs)
        accs = lax.fori_loop(0, nk, body, tuple(jnp.zeros((sub, nq), F32) for _ in range(nacc)))
        acc = (accs[0] + accs[1]) + (accs[2] + accs[3])
        return jnp.broadcast_to(jnp.sum(acc, axis=0, keepdims=True), (sub, nq))

    def vstep(it, lo):
        cand = lo + lax.shift_left(jnp.int32(1), 31 - it)
        cnt = count(lambda k, c, r: k >= cand)
        return jnp.where(cnt >= k_top, cand, lo)

    lo = lax.fori_loop(0, 32, vstep, jnp.full((sub, nq), INT_MIN, I32))
    cnt_gt = count(lambda k, c, r: k > lo)
    cnt_ge = count(lambda k, c, r: k >= lo)
    need = k_top - cnt_gt
    excess = (cnt_ge > k_top) & (lo > NEG_INF_KEY)
    any_excess = jnp.max(excess.astype(I32)) > 0
    eye = (lax.broadcasted_iota(I32, (LANES, LANES), 0)
           == lax.broadcasted_iota(I32, (LANES, LANES), 1)).astype(BF16)

    def write(c, selected):
        for qb in range(nq // LANES):
            qs = slice(qb * LANES, (qb + 1) * LANES)
            sel = selected(key_sc[c, :, qs], qb)
            bias_t = jnp.where(sel, 0.0, NEG_BIG).astype(BF16)
            bias_sc[c, qs, :] = _dot_nt(eye, bias_t)

    @pl.when(jnp.logical_not(any_excess))
    def _():
        thr = jnp.maximum(lo, NEG_INF_KEY + 1)[0:1, :]

        def wbody(c, carry):
            write(c, lambda k, qb: k >= thr[:, qb * LANES:(qb + 1) * LANES])
            return carry

        lax.fori_loop(0, nk, wbody, 0)

    @pl.when(any_excess)
    def _():
        def jstep(it, m):
            cand = m + lax.shift_left(jnp.int32(1), idx_bits - 1 - it)
            f = count(lambda k, c, r: (k == lo) & (key_index(c, r) < cand))
            return jnp.where(f < need, cand, m)
        m = lax.fori_loop(0, idx_bits, jstep, jnp.zeros((sub, nq), I32))
        jmax = jnp.where(excess, m, jnp.int32(2 ** 30))[0:1, :]
        lo1 = lo[0:1, :]

        def wbody(c, carry):
            pos = lax.broadcasted_iota(I32, (ck, LANES), 0) + c * ck

            def selected(k, qb):
                qs = slice(qb * LANES, (qb + 1) * LANES)
                sel = (k > lo1[:, qs]) | ((k == lo1[:, qs]) & (pos <= jmax[:, qs]))
                return sel & (k > NEG_INF_KEY)
            write(c, selected)
            return carry

        lax.fori_loop(0, nk, wbody, 0)


def _flash_step(s, v, m_prev, l_prev, acc_prev):
    nblk = s.shape[1] // LANES
    m_new = jnp.maximum(m_prev, jnp.max(s, axis=1, keepdims=True))
    alpha = jnp.exp(m_prev - m_new)
    p = [jnp.exp(s[:, t * LANES:(t + 1) * LANES] - m_new) for t in range(nblk)]
    psum = p[0]
    for t in range(1, nblk):
        psum = psum + p[t]
    l_new = alpha * l_prev + jnp.sum(psum, axis=1, keepdims=True)
    pb = jnp.concatenate([pt.astype(BF16) for pt in p], axis=1)
    acc_new = alpha * acc_prev + _dot(pb, v)
    return m_new, l_new, acc_new


ATT_CK = 512


def _dsa_prompt_kernel(q_ref, iq_ref, smq_ref, k_ref, v_ref, smk_ref, o_ref,
                       key_sc, bias_sc, m_sc, l_sc, acc_sc, *, n_top, idx_bits):
    qi = pl.program_id(1)
    tq, ck = q_ref.shape[0], ATT_CK
    nk = (qi * tq + tq + ck - 1) // ck
    lane = lax.broadcasted_iota(I32, (tq, LANES), 1)

    iq = iq_ref[...].astype(F32)
    iqh = []
    for p in range(IDX_HEADS // 2):
        grp = iq[:, p * LANES:(p + 1) * LANES]
        iqh.append(jnp.where(lane < IDX_DIM, grp, 0.0).astype(BF16))
        iqh.append(jnp.where(lane < IDX_DIM, pltpu.roll(grp, IDX_DIM, 1), 0.0).astype(BF16))
    smq = smq_ref[...]
    smq_t = jnp.concatenate([smq[b * LANES:(b + 1) * LANES].T for b in range(tq // LANES)], axis=1)
    qpos = qi * tq + lax.broadcasted_iota(I32, (ck, tq), 1)

    def score_body(c, carry):
        kc = smk_ref[pl.ds(pl.multiple_of(c * ck, ck), ck), :]
        sc = jnp.zeros((ck, tq), F32)
        for h in range(IDX_HEADS):
            lg = _dot_nt(kc, iqh[h])
            sc = sc + jnp.maximum(lg, 0.0) * smq_t[SM_IW + h:SM_IW + h + 1, :]
        kpos = c * ck + lax.broadcasted_iota(I32, (ck, tq), 0)
        sc = jnp.where(kpos <= qpos, sc, -jnp.inf)
        key_sc[c] = _sortable(sc)
        return carry

    lax.fori_loop(0, nk, score_body, 0)
    _select_topk_cols(key_sc, bias_sc, nk, n_top, idx_bits)

    m_sc[...] = jnp.full(m_sc.shape, NEG_BIG, F32)
    l_sc[...] = jnp.zeros_like(l_sc)
    acc_sc[...] = jnp.zeros_like(acc_sc)

    def att_body(c, carry):
        off = pl.multiple_of(c * ck, ck)
        bias = bias_sc[c]
        bias2 = jnp.concatenate([bias] * Q_PER_KV, axis=0)
        new = []
        for g in range(KV_HEADS):
            q2 = jnp.concatenate(
                [q_ref[:, (Q_PER_KV * g + t) * HEAD_DIM:(Q_PER_KV * g + t + 1) * HEAD_DIM]
                 for t in range(Q_PER_KV)], axis=0)
            kc = k_ref[pl.ds(off, ck), g * HEAD_DIM:(g + 1) * HEAD_DIM]
            vc = v_ref[pl.ds(off, ck), g * HEAD_DIM:(g + 1) * HEAD_DIM]
            s = _dot_nt(q2, kc) + bias2
            new.append(_flash_step(s, vc, m_sc[g], l_sc[g], acc_sc[g]))
        for g, (m_new, l_new, acc_new) in enumerate(new):
            m_sc[g] = m_new
            l_sc[g] = l_new
            acc_sc[g] = acc_new
        return carry

    lax.fori_loop(0, nk, att_body, 0)
    for g in range(KV_HEADS):
        o = acc_sc[g] / l_sc[g]
        for t in range(Q_PER_KV):
            h = Q_PER_KV * g + t
            o_ref[:, h * HEAD_DIM:(h + 1) * HEAD_DIM] = o[t * tq:(t + 1) * tq].astype(o_ref.dtype)


def _dsa_prompt(yb, ybh, bsz, seq, n_top):
    tq = min(256, seq)
    nq = seq // tq
    nch = seq // ATT_CK
    kernel = functools.partial(_dsa_prompt_kernel, n_top=n_top,
                               idx_bits=max(1, (seq - 1).bit_length()))
    return pl.pallas_call(
        kernel,
        grid=(bsz, nq),
        in_specs=[
            pl.BlockSpec((tq, ATT_DIM), lambda b, i: (b * nq + i, B_Q // ATT_DIM)),
            pl.BlockSpec((tq, 512), lambda b, i: (b * nq + i, B_IQ // 512)),
            pl.BlockSpec((tq, LANES), lambda b, i: (b * nq + i, B_SM // LANES)),
            pl.BlockSpec((seq, KV_DIM), lambda b, i: (b, B_K // KV_DIM)),
            pl.BlockSpec((seq, KV_DIM), lambda b, i: (b, B_V // KV_DIM)),
            pl.BlockSpec((seq, LANES), lambda b, i: (b, B_SM // LANES)),
        ],
        out_specs=pl.BlockSpec((tq, ATT_DIM), lambda b, i: (b * nq + i, 0)),
        out_shape=jax.ShapeDtypeStruct((bsz * seq, ATT_DIM), BF16),
        scratch_shapes=[pltpu.VMEM((nch, ATT_CK, tq), I32),
                        pltpu.VMEM((nch, tq, ATT_CK), F32),
                        pltpu.VMEM((KV_HEADS, Q_PER_KV * tq, LANES), F32),
                        pltpu.VMEM((KV_HEADS, Q_PER_KV * tq, LANES), F32),
                        pltpu.VMEM((KV_HEADS, Q_PER_KV * tq, HEAD_DIM), F32)],
        compiler_params=_cparams("parallel", "arbitrary"),
        name="dsa_prompt",
    )(ybh, ybh, yb, ybh, ybh, ybh)


PAGES_PER_STEP = 32
SCORE_PAGES_PER_STEP = 32


def _sample_score_kernel(pt_ref, iq_ref, sm_ref, smn_ref, *refs, npg):
    del pt_ref
    ik_refs = refs[:npg]
    sp_ref, sn_ref = refs[npg], refs[npg + 1]
    lq = iq_ref.shape[1]
    iq = iq_ref[0]
    sm = sm_ref[0]
    iqh = jnp.concatenate([iq[:, h * IDX_DIM:(h + 1) * IDX_DIM] for h in range(IDX_HEADS)],
                          axis=0).astype(BF16)

    def score(lg):
        sc = jnp.zeros((lq, lg.shape[1]), F32)
        for h in range(IDX_HEADS):
            sc = sc + jnp.maximum(lg[h * lq:(h + 1) * lq], 0.0) * sm[:, SM_IW + h:SM_IW + h + 1]
        return sc

    for i in range(npg):
        sp_ref[0, :, i * PAGE_SIZE:(i + 1) * PAGE_SIZE] = score(
            _dot(iqh, ik_refs[i][0, 0].astype(BF16)))

    @pl.when(pl.program_id(1) == 0)
    def _():
        sc = score(_dot_nt(iqh, smn_ref[0][:, 0:IDX_DIM].astype(BF16)))
        col = lax.broadcasted_iota(I32, sc.shape, 1)
        row = lax.broadcasted_iota(I32, sc.shape, 0)
        sn_ref[0] = jnp.where(col <= row, sc, -jnp.inf)


def _sample_scores(page_table, yb3, sm_pad, pool_ik_t, layer):
    bsz, lq, _ = yb3.shape
    n_pages = page_table.shape[1]
    npg = math.gcd(SCORE_PAGES_PER_STEP, n_pages)
    steps = n_pages // npg
    ik_specs = [pl.BlockSpec((1, 1, IDX_DIM, PAGE_SIZE),
                             functools.partial(
                                 lambda b, j, pt, i: (layer, pt[b, j * npg + i], 0, 0), i=i))
                for i in range(npg)]
    grid_spec = pltpu.PrefetchScalarGridSpec(
        num_scalar_prefetch=1,
        grid=(bsz, steps),
        in_specs=[
            pl.BlockSpec((1, lq, 512), lambda b, j, pt: (b, 0, B_IQ // 512)),
            pl.BlockSpec((1, lq, LANES), lambda b, j, pt: (b, 0, B_SM // LANES)),
            pl.BlockSpec((1, PAGE_SIZE, LANES), lambda b, j, pt: (b, 0, 0)),
        ] + ik_specs,
        out_specs=[pl.BlockSpec((1, lq, npg * PAGE_SIZE), lambda b, j, pt: (b, 0, j)),
                   pl.BlockSpec((1, lq, PAGE_SIZE), lambda b, j, pt: (b, 0, 0))],
    )
    return pl.pallas_call(
        functools.partial(_sample_score_kernel, npg=npg),
        grid_spec=grid_spec,
        out_shape=[jax.ShapeDtypeStruct((bsz, lq, n_pages * PAGE_SIZE), F32),
                   jax.ShapeDtypeStruct((bsz, lq, PAGE_SIZE), F32)],
        compiler_params=_cparams("parallel", "arbitrary"),
        name="sample_scores",
    )(page_table, yb3, yb3, sm_pad, *([pool_ik_t] * npg))


SEL_CK = 512


def _sample_select_kernel(s_ref, o_ref, key_sc, bias_sc, *, n_top, idx_bits):
    nch = key_sc.shape[0]
    for c in range(nch):
        key_sc[c] = _sortable(s_ref[:, c * SEL_CK:(c + 1) * SEL_CK])
    _select_topk(key_sc, bias_sc, nch, n_top, idx_bits)
    for c in range(nch):
        o_ref[:, c * SEL_CK:(c + 1) * SEL_CK] = bias_sc[c]


def _sample_select(scores, n_top):
    rows, width = scores.shape
    nch = width // SEL_CK
    kernel = functools.partial(_sample_select_kernel, n_top=n_top,
                               idx_bits=max(1, (width - 1).bit_length()))
    return pl.pallas_call(
        kernel,
        grid=(1,),
        in_specs=[pl.BlockSpec((rows, width), lambda i: (0, 0))],
        out_specs=pl.BlockSpec((rows, width), lambda i: (0, 0)),
        out_shape=jax.ShapeDtypeStruct((rows, width), F32),
        scratch_shapes=[pltpu.VMEM((nch, rows, SEL_CK), I32),
                        pltpu.VMEM((nch, rows, SEL_CK), F32)],
        compiler_params=_cparams("arbitrary"),
        name="sample_select",
    )(scores)


def _sample_attn_kernel(pt_ref, q_ref, bias_ref, biasn_ref, kn_ref, vn_ref, *refs, npg):
    del pt_ref
    k_refs = refs[:npg]
    v_refs = refs[npg:2 * npg]
    o_ref, m_sc, l_sc, acc_sc = refs[2 * npg:]
    j = pl.program_id(1)
    lq = q_ref.shape[1]

    @pl.when(j == 0)
    def _():
        m_sc[...] = jnp.full(m_sc.shape, NEG_BIG, F32)
        l_sc[...] = jnp.zeros_like(l_sc)
        acc_sc[...] = jnp.zeros_like(acc_sc)

    q = q_ref[0]
    q_all = jnp.concatenate([q[:, h * HEAD_DIM:(h + 1) * HEAD_DIM] for h in range(N_HEADS)],
                            axis=0).astype(BF16)

    def head_rows(ref, g):
        return ref[0, 0, pl.ds(g, PAGE_SIZE, stride=KV_HEADS), :]

    def head_major(page_refs):
        return jnp.concatenate([head_rows(r, g) for g in range(KV_HEADS) for r in page_refs],
                               axis=0).astype(BF16)

    def accumulate(k_pages, v_pages, bias):
        own = jnp.concatenate([bias] * Q_PER_KV, axis=0)
        off = jnp.full(own.shape, NEG_BIG, F32)
        bias_all = jnp.concatenate(
            [jnp.concatenate([own if kh == g else off for kh in range(KV_HEADS)], axis=1)
             for g in range(KV_HEADS)], axis=0)
        s = _dot_nt(q_all, head_major(k_pages)) + bias_all
        m_new, l_new, acc_new = _flash_step(s, head_major(v_pages), m_sc[...], l_sc[...],
                                            acc_sc[...])
        m_sc[...] = m_new
        l_sc[...] = l_new
        acc_sc[...] = acc_new

    accumulate(k_refs, v_refs, bias_ref[0])

    @pl.when(j == pl.num_programs(1) - 1)
    def _():
        accumulate([kn_ref], [vn_ref], biasn_ref[0])
        o = acc_sc[...] / l_sc[...]
        for h in range(N_HEADS):
            o_ref[0, :, h * HEAD_DIM:(h + 1) * HEAD_DIM] = o[h * lq:(h + 1) * lq].astype(o_ref.dtype)


def _sample_attn(page_table, yb3, bias3, kn_pad, vn_pad, pool_k, pool_v, layer):
    bsz, lq, _ = yb3.shape
    n_pages = page_table.shape[1]
    npg = math.gcd(PAGES_PER_STEP, n_pages)
    steps = n_pages // npg
    past = n_pages * PAGE_SIZE
    page_rows = PAGE_SIZE * KV_HEADS

    def page_spec(i):
        return pl.BlockSpec((1, 1, page_rows, HEAD_DIM),
                            functools.partial(
                                lambda b, j, pt, i: (layer, pt[b, j * npg + i], 0, 0), i=i))

    grid_spec = pltpu.PrefetchScalarGridSpec(
        num_scalar_prefetch=1,
        grid=(bsz, steps),
        in_specs=[
            pl.BlockSpec((1, lq, ATT_DIM), lambda b, j, pt: (b, 0, B_Q // ATT_DIM)),
            pl.BlockSpec((1, lq, npg * PAGE_SIZE), lambda b, j, pt: (b, 0, j)),
            pl.BlockSpec((1, lq, PAGE_SIZE), lambda b, j, pt: (b, 0, past // PAGE_SIZE)),
            pl.BlockSpec((1, 1, page_rows, HEAD_DIM), lambda b, j, pt: (b, 0, 0, 0)),
            pl.BlockSpec((1, 1, page_rows, HEAD_DIM), lambda b, j, pt: (b, 0, 0, 0)),
        ] + [page_spec(i) for i in range(npg)] + [page_spec(i) for i in range(npg)],
        out_specs=pl.BlockSpec((1, lq, ATT_DIM), lambda b, j, pt: (b, 0, 0)),
        scratch_shapes=[pltpu.VMEM((N_HEADS * lq, LANES), F32),
                        pltpu.VMEM((N_HEADS * lq, LANES), F32),
                        pltpu.VMEM((N_HEADS * lq, HEAD_DIM), F32)],
    )
    return pl.pallas_call(
        functools.partial(_sample_attn_kernel, npg=npg),
        grid_spec=grid_spec,
        out_shape=jax.ShapeDtypeStruct((bsz, lq, ATT_DIM), BF16),
        compiler_params=_cparams("parallel", "arbitrary"),
        name="sample_attn",
    )(page_table, yb3, bias3, bias3, kn_pad, vn_pad, *([pool_k] * npg), *([pool_v] * npg))


def _ssd_kernel(xs_ref, z_ref, bm_ref, cm_ref, sm_ref, cs_ref, h0_ref, cw_ref, cb_ref,
                dtb_ref, aneg_ref, dsk_ref, nrm_ref, exp_ref,
                y_ref, hout_ref, xp_sc, sm_sc, ht_sc, *, rows, has_state, lead):
    c = pl.program_id(1)
    lc = SSD_CHUNK
    pad = 8
    hist = CONV_W - 1

    def blk(ref):
        return ref[0] if lead else ref[...]

    @pl.when(c == 0)
    def _():
        xp_sc[...] = jnp.zeros_like(xp_sc)
        sm_sc[...] = jnp.zeros_like(sm_sc)
        if has_state:
            xp_sc[pad - hist:pad, :] = cs_ref[0, 0]
            for jp in range(SSD_HEADS // 2):
                pair = jnp.concatenate([h0_ref[0, 0, 2 * jp], h0_ref[0, 0, 2 * jp + 1]], axis=0)
                ht_sc[:, jp * LANES:(jp + 1) * LANES] = pair.T
        else:
            ht_sc[...] = jnp.zeros_like(ht_sc)

    @pl.when(c > 0)
    def _():
        xp_sc[0:pad, :] = xp_sc[lc:lc + pad, :]

    xp_sc[pad:pad + rows, 0:D_INNER] = blk(xs_ref)
    xp_sc[pad:pad + rows, D_INNER:D_INNER + BC_DIM] = blk(bm_ref)
    xp_sc[pad:pad + rows, D_INNER + BC_DIM:CONV_CH] = blk(cm_ref)
    sm_sc[0:rows, :] = blk(sm_ref)

    xe = xp_sc[...]
    conv = cw_ref[0:1, :] * xe
    for i in range(1, CONV_W):
        conv = pltpu.roll(conv, 1, 0) + cw_ref[i:i + 1, :] * xe
    conv = _silu(conv[pad:pad + lc] + cb_ref[...])
    xs = conv[:, 0:D_INNER]
    bmat = conv[:, D_INNER:D_INNER + BC_DIM]
    cmat = conv[:, D_INNER + BC_DIM:CONV_CH]

    lane = lax.broadcasted_iota(I32, (lc, LANES), 1)
    row = lax.broadcasted_iota(I32, (lc, LANES), 0)
    dt_lanes = (lane >= SM_DT) & (lane < SM_DT + SSD_HEADS) & (row < rows)
    raw = sm_sc[...] + dtb_ref[...]
    softplus = jnp.maximum(raw, 0.0) + jnp.log(1.0 + jnp.exp(-jnp.abs(raw)))
    dt = jnp.where(dt_lanes, softplus, 0.0)
    a = dt * aneg_ref[...]
    ltri = (row >= lane).astype(F32)
    acum = jnp.dot(ltri, a, preferred_element_type=F32, precision=lax.Precision.HIGHEST)
    acum_t = acum.T
    dt_t = dt.T
    a_last = acum[lc - 1:lc, :]
    w_end = jnp.exp(a_last - acum) * dt
    expand = exp_ref[...]
    w_exp = _dot(w_end.astype(BF16), expand)
    e_exp = _dot(jnp.exp(acum).astype(BF16), expand)
    cd = jnp.broadcast_to(jnp.exp(a_last), (8, LANES))
    cd_hi = cd.astype(BF16)
    cd_r = cd - cd_hi.astype(F32)
    cd_mid = cd_r.astype(BF16)
    cd_lo = (cd_r - cd_mid.astype(F32)).astype(BF16)
    cd_exp = ((_dot(cd_hi, expand) + _dot(cd_mid, expand)) + _dot(cd_lo, expand))[0:1, :]
    xw = (xs * w_exp).astype(BF16)
    xs_b = xs.astype(BF16)
    causal = row >= lane

    ht_prev = ht_sc[...]
    gw = D_INNER // SSD_GROUPS
    heads_pg = SSD_HEADS // SSD_GROUPS
    y_parts = []
    for g in range(SSD_GROUPS):
        bg = bmat[:, g * D_STATE:(g + 1) * D_STATE]
        cg = cmat[:, g * D_STATE:(g + 1) * D_STATE].astype(BF16)
        cb = _dot_nt(cg, bg.astype(BF16))
        y_off = _dot(cg, ht_prev[:, g * gw:(g + 1) * gw].astype(BF16)) * e_exp[:, g * gw:(g + 1) * gw]
        states_t = _dot(bg.T.astype(BF16), xw[:, g * gw:(g + 1) * gw])
        ht_sc[:, g * gw:(g + 1) * gw] = (ht_prev[:, g * gw:(g + 1) * gw] * cd_exp[:, g * gw:(g + 1) * gw]
                                         + states_t)
        diag = []
        for jp in range(heads_pg // 2):
            ms = []
            for t in range(2):
                h = g * heads_pg + 2 * jp + t
                col = acum[:, SM_DT + h:SM_DT + h + 1]
                rowv = acum_t[SM_DT + h:SM_DT + h + 1, :]
                dec = jnp.exp(jnp.where(causal, col - rowv, NEG_BIG))
                ms.append((cb * dec * dt_t[SM_DT + h:SM_DT + h + 1, :]).astype(BF16))
            mpair = jnp.concatenate(ms, axis=1)
            xpair = xs_b[:, g * gw + jp * LANES:g * gw + (jp + 1) * LANES]
            zero = jnp.zeros_like(xpair)
            xbd = jnp.concatenate([jnp.where(lane < SSD_HEAD_DIM, xpair, zero),
                                   jnp.where(lane >= SSD_HEAD_DIM, xpair, zero)], axis=0)
            diag.append(_dot(mpair, xbd))
        y_parts.append(jnp.concatenate(diag, axis=1) + y_off)
    y = jnp.concatenate(y_parts, axis=1) + dsk_ref[...] * xs

    zval = blk(z_ref)
    if rows < lc:
        zval = jnp.concatenate([zval, jnp.zeros((lc - rows, D_INNER), F32)], axis=0)
    yz = y * _silu(zval)
    outs = []
    for g in range(SSD_GROUPS):
        part = yz[:, g * gw:(g + 1) * gw]
        part = part * lax.rsqrt(jnp.mean(part * part, axis=-1, keepdims=True) + EPS)
        outs.append(part * nrm_ref[:, g * gw:(g + 1) * gw])
    out = jnp.concatenate(outs, axis=1)
    if lead:
        y_ref[0] = out[0:rows].astype(y_ref.dtype)
    else:
        y_ref[...] = out.astype(y_ref.dtype)

    @pl.when(c == pl.num_programs(1) - 1)
    def _():
        for jp in range(SSD_HEADS // 2):
            pair = ht_sc[:, jp * LANES:(jp + 1) * LANES].T
            hout_ref[0, 2 * jp] = pair[0:SSD_HEAD_DIM]
            hout_ref[0, 2 * jp + 1] = pair[SSD_HEAD_DIM:]


def _ssd_consts(p, l):
    pad_l = SM_DT
    pad_r = LANES - SM_DT - SSD_HEADS

    def lanes(v):
        return jnp.pad(v.astype(F32), (pad_l, pad_r)).reshape(1, LANES)

    a_neg = -jnp.exp(p['a_log'][l].astype(F32))
    heads = jnp.arange(D_INNER, dtype=I32) // SSD_HEAD_DIM
    expand = (jnp.arange(LANES, dtype=I32)[:, None] == heads[None, :] + SM_DT).astype(BF16)
    return dict(
        cw=p['conv_w'][l], cb=p['conv_b'][l].reshape(1, CONV_CH),
        dtb=lanes(p['dt_bias'][l]), aneg=lanes(a_neg),
        dsk=jnp.repeat(p['d_skip'][l].astype(F32), SSD_HEAD_DIM).reshape(1, D_INNER),
        nrm=p['norm_ssd'][l].reshape(1, D_INNER), expand=expand)


def _ssd_prompt(ya, yb, bsz, seq, k):
    nc = seq // SSD_CHUNK
    lc = SSD_CHUNK
    const = lambda shape: pl.BlockSpec(shape, lambda b, c: (0,) * len(shape))
    dummy_cs = jnp.zeros((1, 1, CONV_W - 1, CONV_CH), F32)
    dummy_h0 = jnp.zeros((1, 1, 2, SSD_HEAD_DIM, D_STATE), F32)
    return pl.pallas_call(
        functools.partial(_ssd_kernel, rows=lc, has_state=False, lead=False),
        grid=(bsz, nc),
        in_specs=[
            pl.BlockSpec((lc, D_INNER), lambda b, c: (b * nc + c, A_XS // D_INNER)),
            pl.BlockSpec((lc, D_INNER), lambda b, c: (b * nc + c, A_Z // D_INNER)),
            pl.BlockSpec((lc, BC_DIM), lambda b, c: (b * nc + c, A_BM // BC_DIM)),
            pl.BlockSpec((lc, BC_DIM), lambda b, c: (b * nc + c, A_CM // BC_DIM)),
            pl.BlockSpec((lc, LANES), lambda b, c: (b * nc + c, B_SM // LANES)),
            const((1, 1, CONV_W - 1, CONV_CH)),
            const((1, 1, 2, SSD_HEAD_DIM, D_STATE)),
            const((CONV_W, CONV_CH)), const((1, CONV_CH)),
            const((1, LANES)), const((1, LANES)),
            const((1, D_INNER)), const((1, D_INNER)), const((LANES, D_INNER)),
        ],
        out_specs=[pl.BlockSpec((lc, D_INNER), lambda b, c: (b * nc + c, 0)),
                   pl.BlockSpec((1, SSD_HEADS, SSD_HEAD_DIM, D_STATE), lambda b, c: (b, 0, 0, 0))],
        out_shape=[jax.ShapeDtypeStruct((bsz * seq, D_INNER), BF16),
                   jax.ShapeDtypeStruct((bsz, SSD_HEADS, SSD_HEAD_DIM, D_STATE), F32)],
        scratch_shapes=[pltpu.VMEM((lc + 8, CONV_CH), F32), pltpu.VMEM((lc, LANES), F32),
                        pltpu.VMEM((D_STATE, D_INNER), F32)],
        compiler_params=_cparams("parallel", "arbitrary"),
        name="ssd_prompt",
    )(ya, ya, ya, ya, yb, dummy_cs, dummy_h0, k['cw'], k['cb'], k['dtb'], k['aneg'],
      k['dsk'], k['nrm'], k['expand'])


def _ssd_sample(ya3, yb3, conv_state, ssm_state, k, layer):
    bsz, lq, _ = ya3.shape
    lc = SSD_CHUNK
    const = lambda shape: pl.BlockSpec(shape, lambda b, c: (0,) * len(shape))
    return pl.pallas_call(
        functools.partial(_ssd_kernel, rows=lq, has_state=True, lead=True),
        grid=(bsz, 1),
        in_specs=[
            pl.BlockSpec((1, lq, D_INNER), lambda b, c: (b, 0, A_XS // D_INNER)),
            pl.BlockSpec((1, lq, D_INNER), lambda b, c: (b, 0, A_Z // D_INNER)),
            pl.BlockSpec((1, lq, BC_DIM), lambda b, c: (b, 0, A_BM // BC_DIM)),
            pl.BlockSpec((1, lq, BC_DIM), lambda b, c: (b, 0, A_CM // BC_DIM)),
            pl.BlockSpec((1, lq, LANES), lambda b, c: (b, 0, B_SM // LANES)),
            pl.BlockSpec((1, 1, CONV_W - 1, CONV_CH), lambda b, c: (layer, b, 0, 0)),
            pl.BlockSpec((1, 1, SSD_HEADS, SSD_HEAD_DIM, D_STATE),
                         lambda b, c: (layer, b, 0, 0, 0)),
            const((CONV_W, CONV_CH)), const((1, CONV_CH)),
            const((1, LANES)), const((1, LANES)),
            const((1, D_INNER)), const((1, D_INNER)), const((LANES, D_INNER)),
        ],
        out_specs=[pl.BlockSpec((1, lq, D_INNER), lambda b, c: (b, 0, 0)),
                   pl.BlockSpec((1, SSD_HEADS, SSD_HEAD_DIM, D_STATE), lambda b, c: (b, 0, 0, 0))],
        out_shape=[jax.ShapeDtypeStruct((bsz, lq, D_INNER), BF16),
                   jax.ShapeDtypeStruct((bsz, SSD_HEADS, SSD_HEAD_DIM, D_STATE), F32)],
        scratch_shapes=[pltpu.VMEM((lc + 8, CONV_CH), F32), pltpu.VMEM((lc, LANES), F32),
                        pltpu.VMEM((D_STATE, D_INNER), F32)],
        compiler_params=_cparams("parallel", "arbitrary"),
        name="ssd_sample",
    )(ya3, ya3, ya3, ya3, yb3, conv_state, ssm_state, k['cw'], k['cb'], k['dtb'], k['aneg'],
      k['dsk'], k['nrm'], k['expand'])


def _merge_kernel(x_ref, gt_ref, at_ref, sd_ref, ga_ref, gb_ref, wa_ref, wb_ref, wo_ref, o_ref):
    ma = _dot(at_ref[...], wa_ref[...])
    mb = _dot(sd_ref[...], wb_ref[...])
    merged = jax.nn.sigmoid(ga_ref[...]) * ma + jax.nn.sigmoid(gb_ref[...]) * mb
    o_ref[...] = x_ref[...] + gt_ref[0] * _dot(merged.astype(BF16), wo_ref[...])


def _merge(x, gt, attn, ssd, ya, wa, wb, wo, *, tm):
    t, d = x.shape
    groups, r, _ = gt.shape
    tpg = (t // groups) // tm
    return pl.pallas_call(
        _merge_kernel,
        grid=(t // tm,),
        in_specs=[
            pl.BlockSpec((tm, d), lambda i: (i, 0)),
            pl.BlockSpec((1, r, d), lambda i: (i // tpg, 0, 0)),
            pl.BlockSpec((tm, ATT_DIM), lambda i: (i, 0)),
            pl.BlockSpec((tm, D_INNER), lambda i: (i, 0)),
            pl.BlockSpec((tm, d), lambda i: (i, A_GA // D_MODEL)),
            pl.BlockSpec((tm, d), lambda i: (i, A_GB // D_MODEL)),
            _resident((ATT_DIM, d)), _resident((D_INNER, d)), _resident((d, d)),
        ],
        out_specs=pl.BlockSpec((tm, d), lambda i: (i, 0)),
        out_shape=jax.ShapeDtypeStruct((t, d), F32),
        compiler_params=_cparams("parallel"),
        name="merge",
    )(x, gt, attn, ssd, ya, ya, wa, wb, wo)


def _layer_weights(p, l):
    w_in = p['w_in'][l]
    offs = {}
    o = 0
    for name, size in (('q', ATT_DIM), ('k', KV_DIM), ('v', KV_DIM), ('iq', IDX_HEADS * IDX_DIM),
                       ('ik', IDX_DIM), ('iw', IDX_HEADS), ('z', D_INNER), ('xbc', CONV_CH),
                       ('dt', SSD_HEADS), ('ga', D_MODEL), ('gb', D_MODEL)):
        offs[name] = (o, o + size)
        o += size

    def seg(name):
        s, e = offs[name]
        return w_in[:, s:e]

    xbc = seg('xbc')
    w_a = jnp.concatenate([xbc[:, :D_INNER], seg('z'), seg('ga'), seg('gb'),
                           xbc[:, D_INNER:D_INNER + BC_DIM], xbc[:, D_INNER + BC_DIM:]], axis=1)
    small = jnp.concatenate([seg('ik'), seg('iw'), seg('dt')], axis=1)
    small = jnp.pad(small, ((0, 0), (0, B_DIM - B_SM - small.shape[1])))
    w_b = jnp.concatenate([seg('q'), seg('k'), seg('v'), seg('iq'), small], axis=1)
    return dict(
        w_a=w_a.astype(BF16), w_b=w_b.astype(BF16),
        up1=p['w_ffn1_up'][l].astype(BF16), down1=p['w_ffn1_down'][l].astype(BF16),
        up2=p['w_ffn2_up'][l].astype(BF16), down2=p['w_ffn2_down'][l].astype(BF16),
        wa=p['w_br_a'][l].astype(BF16), wb=p['w_br_b'][l].astype(BF16), wo=p['w_out'][l].astype(BF16),
        g1=p['norm_ffn1'][l].reshape(1, D_MODEL), gm=p['norm_mix'][l].reshape(1, D_MODEL),
        g2=p['norm_ffn2'][l].reshape(1, D_MODEL))


TM_FFN, TM_PROJ_A, TM_PROJ_B, TM_MERGE = 512, 256, 512, 256


def _pre_mix(x, mods, w, tables, rows):
    sh1, sc1, g1, sh2, sc2 = mods[0], mods[1], mods[2], mods[3], mods[4]
    x = _ffn(x, sh1, sc1, g1, w['g1'], w['g1'], w['up1'], w['down1'],
             tm=min(TM_FFN, rows), final_norm=False)
    ya = _proj_a(x, sh2, sc2, w['gm'], w['w_a'], tm=min(TM_PROJ_A, rows))
    yb, ybh, k_rows, v_rows = _proj_b(x, sh2, sc2, w['gm'], w['w_b'], tables[0], tables[1],
                                      tm=min(TM_PROJ_B, rows))
    return x, ya, yb, ybh, k_rows, v_rows


def _post_mix(x, mods, w, attn, ssd, ya, gf, rows, final_norm):
    x = _merge(x, mods[5], attn, ssd, ya, w['wa'], w['wb'], w['wo'], tm=min(TM_MERGE, rows))
    return _ffn(x, mods[6], mods[7], mods[8], w['g2'], gf, w['up2'], w['down2'],
                tm=min(TM_FFN, rows), final_norm=final_norm)


def kernel(x_prompt, x_sample, cache_k, cache_v, cache_idx_k, state_ssm, state_conv, page_table,
           c_prompt, c_sample, w_ada, b_ada, norm_ffn1, w_ffn1_up, w_ffn1_down, norm_mix, w_in,
           conv_w, conv_b, dt_bias, a_log, d_skip, norm_ssd, w_br_a, w_br_b, w_out,
           norm_ffn2, w_ffn2_up, w_ffn2_down, norm_final):
    p = {'w_ada': w_ada, 'b_ada': b_ada, 'norm_ffn1': norm_ffn1, 'w_ffn1_up': w_ffn1_up,
         'w_ffn1_down': w_ffn1_down, 'norm_mix': norm_mix, 'w_in': w_in, 'conv_w': conv_w,
         'conv_b': conv_b, 'dt_bias': dt_bias, 'a_log': a_log, 'd_skip': d_skip,
         'norm_ssd': norm_ssd, 'w_br_a': w_br_a, 'w_br_b': w_br_b, 'w_out': w_out,
         'norm_ffn2': norm_ffn2, 'w_ffn2_up': w_ffn2_up, 'w_ffn2_down': w_ffn2_down}
    depth = w_in.shape[0]
    bsz, seq, d = x_prompt.shape
    dbsz, dseq, _ = x_sample.shape
    n_pages = page_table.shape[1]
    past_len = n_pages * PAGE_SIZE
    n_pool = cache_k.shape[1]
    tp = bsz * seq
    ts = dbsz * dseq

    rows = bsz + dbsz
    rpad = -rows % 8
    c_all = jnp.pad(jnp.concatenate([c_prompt, c_sample], axis=0), ((0, rpad), (0, 0)))
    mods_all = _ada(c_all, w_ada, b_ada)

    def mods_for(l):
        mp, ms = [], []
        for i in range(N_MODS):
            m = mods_all[l, :, i * d:(i + 1) * d]
            mp.append(m[:bsz].reshape(bsz, 1, d))
            ms.append(jnp.repeat(m[bsz:rows], dseq, axis=0).reshape(1, ts, d))
        return mp, ms

    tab_p = _rope_tables(jnp.arange(seq, dtype=I32))
    tab_s = _rope_tables(past_len + jnp.tile(jnp.arange(dseq, dtype=I32), dbsz))
    gf = norm_final.reshape(1, d)
    n_top_p = min(INDEX_TOPK, seq // 4)
    n_top_s = min(INDEX_TOPK, (past_len + dseq) // 4)

    pool_k = cache_k.reshape(depth, n_pool, PAGE_SIZE * KV_HEADS, HEAD_DIM)
    pool_v = cache_v.reshape(depth, n_pool, PAGE_SIZE * KV_HEADS, HEAD_DIM)
    pool_ik_t = jnp.swapaxes(cache_idx_k, 2, 3)

    xp = x_prompt.reshape(tp, d)
    xs = x_sample.reshape(ts, d)
    outs = {k: [] for k in ('kp', 'vp', 'ikp', 'sp', 'cp', 'ks', 'vs', 'iks', 'ss', 'cs')}
    for l in range(depth):
        w = _layer_weights(p, l)
        kc = _ssd_consts(p, l)
        mp, ms = mods_for(l)
        last = l == depth - 1

        xp, ya, yb, ybh, k_rows, v_rows = _pre_mix(xp, mp, w, tab_p, seq)
        attn = _dsa_prompt(yb, ybh, bsz, seq, n_top_p)
        ssd, h_fin = _ssd_prompt(ya, yb, bsz, seq, kc)
        xp = _post_mix(xp, mp, w, attn, ssd, ya, gf, seq, last)
        yb4 = yb.reshape(bsz, seq, B_DIM)
        ya4 = ya.reshape(bsz, seq, A_DIM)
        outs['kp'].append(k_rows.reshape(bsz, seq, KV_HEADS, HEAD_DIM))
        outs['vp'].append(v_rows.reshape(bsz, seq, KV_HEADS, HEAD_DIM))
        outs['ikp'].append(yb4[:, :, B_SM:B_SM + IDX_DIM])
        outs['sp'].append(h_fin)
        tail = ya4[:, seq - (CONV_W - 1):, :]
        outs['cp'].append(jnp.concatenate(
            [tail[:, :, A_XS:A_XS + D_INNER], tail[:, :, A_BM:A_BM + BC_DIM],
             tail[:, :, A_CM:A_CM + BC_DIM]], axis=-1))

        xs, ya, yb, _, k_rows, v_rows = _pre_mix(xs, ms, w, tab_s, ts)
        ya3 = ya.reshape(dbsz, dseq, A_DIM)
        yb3 = yb.reshape(dbsz, dseq, B_DIM)
        rpad_s = PAGE_SIZE - dseq
        sm_pad = jnp.pad(yb3[:, :, B_SM:B_SM + LANES], ((0, 0), (0, rpad_s), (0, 0)))
        page_rows = PAGE_SIZE * KV_HEADS
        new_rows = dseq * KV_HEADS
        kn_pad = jnp.pad(k_rows.reshape(dbsz, 1, new_rows, HEAD_DIM),
                         ((0, 0), (0, 0), (0, page_rows - new_rows), (0, 0)))
        vn_pad = jnp.pad(v_rows.reshape(dbsz, 1, new_rows, HEAD_DIM),
                         ((0, 0), (0, 0), (0, page_rows - new_rows), (0, 0)))
        s_past, s_new = _sample_scores(page_table, yb3, sm_pad, pool_ik_t, l)
        s_fill = jnp.full((dbsz, dseq, SEL_CK - PAGE_SIZE), -jnp.inf, F32)
        scores = jnp.concatenate([s_past, s_new, s_fill], axis=-1).reshape(ts, past_len + SEL_CK)
        bias = _sample_select(scores, n_top_s).reshape(dbsz, dseq, past_len + SEL_CK)
        attn = _sample_attn(page_table, yb3, bias, kn_pad, vn_pad, pool_k, pool_v, l)
        ssd, h_fin = _ssd_sample(ya3, yb3, state_conv, state_ssm, kc, l)
        xs = _post_mix(xs, ms, w, attn.reshape(ts, ATT_DIM), ssd.reshape(ts, D_INNER), ya, gf,
                       ts, last)
        outs['ks'].append(k_rows.reshape(dbsz, dseq, KV_HEADS, HEAD_DIM))
        outs['vs'].append(v_rows.reshape(dbsz, dseq, KV_HEADS, HEAD_DIM))
        outs['iks'].append(yb3[:, :, B_SM:B_SM + IDX_DIM])
        outs['ss'].append(h_fin)
        xbc = jnp.concatenate([ya3[:, :, A_XS:A_XS + D_INNER], ya3[:, :, A_BM:A_BM + BC_DIM],
                               ya3[:, :, A_CM:A_CM + BC_DIM]], axis=-1)
        xpad = jnp.concatenate([state_conv[l], xbc], axis=1)
        outs['cs'].append(xpad[:, dseq:])

    return (xp.reshape(bsz, seq, d), xs.reshape(dbsz, dseq, d),
            jnp.stack(outs['kp']), jnp.stack(outs['vp']), jnp.stack(outs['ikp']),
            jnp.stack(outs['sp']), jnp.stack(outs['cp']),
            jnp.stack(outs['ks']), jnp.stack(outs['vs']), jnp.stack(outs['iks']),
            jnp.stack(outs['ss']), jnp.stack(outs['cs']))
```
